```python
import functools
import jax, jax.numpy as jnp
from jax import lax
import numpy as np

D_MODEL = 1024
BATCH = 1
SEQ = 16384
DEPTH = 1
DEC_BATCH = 128
DEC_SEQ = 8
PAST_LEN = 16384
PAGE_SIZE = 128

N_META = 16
N_HEADS = 16
HEAD_DIM = 64
N_KV_HEADS = 4
GROUP = N_HEADS // N_KV_HEADS
WINDOW = 128
ATTN_WIDTH = N_HEADS * HEAD_DIM
KV_WIDTH = N_KV_HEADS * HEAD_DIM
HG_EXPAND = 128
HG_HEADS = D_MODEL // HG_EXPAND
HG_KDIM = HG_EXPAND
HG_VDIM = D_MODEL // HG_HEADS
HG_FWIDTH = HG_HEADS * HG_KDIM
HG_VWIDTH = HG_HEADS * HG_VDIM
HG_CHUNK = 64
D_FF = 2816
CONV_W = 3
SPLIT_SIZES = (ATTN_WIDTH, KV_WIDTH, KV_WIDTH, HG_FWIDTH, HG_FWIDTH, HG_VWIDTH, HG_VWIDTH, D_MODEL, D_MODEL)
IN_COLS = sum(SPLIT_SIZES)
DEEPNORM_ALPHA = (2.0 * DEPTH) ** 0.25
DEEPNORM_BETA = (8.0 * DEPTH) ** -0.25
LN_EPS = 1e-5
RMS_EPS = 1e-6
NEG_INF = -1e30

kernel_name = 'hybrid_swa_sink_hgrn2_convffn_step'


def layer_norm(x, g, b):
    xf = x.astype(jnp.float32)
    mu = jnp.mean(xf, axis=-1, keepdims=True)
    var = jnp.mean(jnp.square(xf - mu), axis=-1, keepdims=True)
    return ((xf - mu) * lax.rsqrt(var + LN_EPS) * g.astype(jnp.float32) + b.astype(jnp.float32)).astype(x.dtype)


def rms_norm(x, g):
    xf = x.astype(jnp.float32)
    return xf * lax.rsqrt(jnp.mean(xf * xf, axis=-1, keepdims=True) + RMS_EPS) * g.astype(jnp.float32)


def split_columns(z):
    offsets = np.cumsum(SPLIT_SIZES)[:-1].tolist()
    return jnp.split(z, offsets, axis=-1)


def sink_attention(q, k, v, mask, sinks):
    s = jnp.einsum('...tkgd,...skd->...kgts', q, k, preferred_element_type=jnp.float32) * (HEAD_DIM ** -0.5)
    s = jnp.where(mask, s, NEG_INF)
    sink = jnp.broadcast_to(sinks.astype(jnp.float32).reshape(N_KV_HEADS, GROUP, 1, 1), s.shape[:-1] + (1,))
    p = jax.nn.softmax(jnp.concatenate([s, sink], axis=-1), axis=-1)[..., :-1]
    return jnp.einsum('...kgts,...skd->...tkgd', p.astype(v.dtype), v)


def swa_prompt(q, k, v, sinks):
    B, L = q.shape[:2]
    pad = (-N_META) % WINDOW
    nb = (L + pad) // WINDOW
    padt = lambda a: jnp.pad(a, ((0, 0), (pad, 0), (0, 0), (0, 0)))
    qb = padt(q).reshape(B, nb, WINDOW, N_KV_HEADS, GROUP, HEAD_DIM)

    def band_keys(a):
        blocks = padt(a).reshape(B, nb, WINDOW, N_KV_HEADS, HEAD_DIM)
        prev = jnp.pad(blocks, ((0, 0), (1, 0), (0, 0), (0, 0), (0, 0)))[:, :-1]
        meta = jnp.broadcast_to(a[:, None, :N_META], (B, nb, N_META, N_KV_HEADS, HEAD_DIM))
        return jnp.concatenate([meta, prev, blocks], axis=2)

    kk = band_keys(k)
    vv = band_keys(v)
    qpos = (jnp.arange(nb * WINDOW) - pad).reshape(nb, WINDOW)
    qp = qpos[:, :, None]
    kpos = jnp.concatenate([qpos - WINDOW, qpos], axis=1)[:, None, :]
    band_ok = (kpos >= N_META) & (kpos <= qp) & (kpos > qp - WINDOW)
    meta_ok = jnp.arange(N_META)[None, None, :] <= qp
    mask = jnp.concatenate([meta_ok, band_ok], axis=-1)
    o = sink_attention(qb, kk, vv, mask[None, :, None, None], sinks)
    o = o.reshape(B, nb * WINDOW, ATTN_WIDTH)[:, pad:]
    return o, (k[:, :N_META], v[:, :N_META], k[:, -WINDOW:], v[:, -WINDOW:])


def swa_sample(q, k, v, meta_k, meta_v, win_k, win_v, sinks):
    B, T = q.shape[:2]
    band_k = jnp.concatenate([win_k.astype(k.dtype), k], axis=1)
    band_v = jnp.concatenate([win_v.astype(v.dtype), v], axis=1)
    kk = jnp.concatenate([meta_k.astype(k.dtype), band_k], axis=1)
    vv = jnp.concatenate([meta_v.astype(v.dtype), band_v], axis=1)
    j = jnp.arange(T)[:, None]
    idx = jnp.arange(WINDOW + T)[None, :]
    band_ok = (idx > j) & (idx <= WINDOW + j) & (PAST_LEN - WINDOW + idx >= N_META)
    mask = jnp.concatenate([jnp.ones((T, N_META), bool), band_ok], axis=1)
    o = sink_attention(q.reshape(B, T, N_KV_HEADS, GROUP, HEAD_DIM), kk, vv, mask, sinks)
    return o.reshape(B, T, ATTN_WIDTH), (band_k[:, -WINDOW:], band_v[:, -WINDOW:])


def hgrn_gates(hq, hf, hi, lb):
    B, T = hq.shape[:2]
    f = lb + (1.0 - lb) * jax.nn.sigmoid(hf.astype(jnp.float32))
    heads = lambda a, d: a.reshape(B, T, HG_HEADS, d)
    q = heads(jax.nn.silu(hq.astype(jnp.float32)), HG_KDIM)
    k = heads(1.0 - f, HG_KDIM)
    log_f = heads(jnp.log(f), HG_KDIM)
    v = heads(hi.astype(jnp.float32), HG_VDIM)
    return q, k, v, log_f


def hgrn_prompt(q, k, v, log_f):
    B, L = q.shape[:2]
    pad = (-N_META) % HG_CHUNK
    padt = lambda a: jnp.pad(a, ((0, 0), (pad, 0), (0, 0), (0, 0)))
    nc = (L + pad) // HG_CHUNK
    rs = lambda a: padt(a).reshape(B, nc, HG_CHUNK, HG_HEADS, a.shape[-1])
    q, k, v, log_f = rs(q), rs(k), rs(v), rs(log_f)
    b = jnp.cumsum(log_f, axis=2)
    b_last = b[:, :, -1:]
    q_d = q * jnp.exp(b)
    k_d = k * jnp.exp(-b)
    k_end = k * jnp.exp(b_last - b)
    causal = jnp.tril(jnp.ones((HG_CHUNK, HG_CHUNK), bool))
    scores = jnp.where(causal, jnp.einsum('bnthd,bnshd->bnhts', q_d, k_d), 0.0)
    o_intra = jnp.einsum('bnhts,bnshv->bnthv', scores, v)
    chunk_state = jnp.einsum('bnshd,bnshv->bnhdv', k_end, v)
    decay = jnp.exp(b_last[:, :, 0])

    def step(S, inp):
        dec, cs = inp
        return dec[..., None] * S + cs, S

    S0 = jnp.zeros((B, HG_HEADS, HG_KDIM, HG_VDIM), jnp.float32)
    S_final, S_prev = lax.scan(step, S0, (jnp.swapaxes(decay, 0, 1), jnp.swapaxes(chunk_state, 0, 1)))
    o_inter = jnp.einsum('bnthd,bnhdv->bnthv', q_d, jnp.swapaxes(S_prev, 0, 1))
    o = (o_intra + o_inter).reshape(B, nc * HG_CHUNK, HG_HEADS, HG_VDIM)[:, pad:]
    return o, S_final


def hgrn_sample(q, k, v, log_f, s0):
    def step(S, inp):
        q_t, k_t, v_t, lf_t = inp
        S = jnp.exp(lf_t)[..., None] * S + k_t[..., None] * v_t[..., None, :]
        return S, jnp.einsum('bhd,bhdv->bhv', q_t, S)

    tm = lambda a: jnp.swapaxes(a, 0, 1)
    S, o = lax.scan(step, s0.astype(jnp.float32), (tm(q), tm(k), tm(v), tm(log_f)))
    return tm(o), S


def conv_ffn(x, conv_buf, w_up_l, conv_w_l, conv_b_l, w_down_l):
    T = x.shape[1]
    u, val = jnp.split(x @ w_up_l, 2, axis=-1)
    u_ext = jnp.concatenate([conv_buf.astype(u.dtype), u], axis=1)
    c = conv_b_l + sum(conv_w_l[j] * u_ext[:, j:j + T] for j in range(CONV_W))
    h = jax.nn.gelu(c) * val
    return h @ w_down_l, u_ext[:, -(CONV_W - 1):]


def trunk_layer(x, attn_fn, hgrn_fn, conv_buf, lb, w_in_l, norm_g_l, w_out_l, ln1_g_l, ln1_b_l,
                w_up_l, conv_w_l, conv_b_l, w_down_l, ln2_g_l, ln2_b_l):
    B, T = x.shape[:2]
    qa, ka, va, hq, hf, hi, hg, ga, gb = split_columns(x @ w_in_l)
    o_attn, attn_state = attn_fn(qa.reshape(B, T, N_HEADS, HEAD_DIM),
                                 ka.reshape(B, T, N_KV_HEADS, HEAD_DIM),
                                 va.reshape(B, T, N_KV_HEADS, HEAD_DIM))
    q, k, v, log_f = hgrn_gates(hq, hf, hi, lb)
    o_hg, hg_state = hgrn_fn(q, k, v, log_f)
    o_hg = rms_norm(o_hg.reshape(B, T, HG_VWIDTH), norm_g_l).astype(x.dtype) * jax.nn.silu(hg)
    merged = jax.nn.sigmoid(ga) * o_attn + jax.nn.sigmoid(gb) * o_hg
    x = layer_norm(DEEPNORM_ALPHA * x + merged @ w_out_l, ln1_g_l, ln1_b_l)
    ffn_out, conv_state = conv_ffn(x, conv_buf, w_up_l, conv_w_l, conv_b_l, w_down_l)
    x = layer_norm(DEEPNORM_ALPHA * x + ffn_out, ln2_g_l, ln2_b_l)
    return x, attn_state, hg_state, conv_state


def setup_inputs(seed: int = 0) -> dict:
    key = jax.random.key(seed)
    ks = jax.random.split(key, 22)

    def nrm(k, shape, scale):
        return jax.random.normal(k, shape, jnp.float32) * scale

    meta_shape = (DEPTH, DEC_BATCH, N_META, N_KV_HEADS, HEAD_DIM)
    win_shape = (DEPTH, DEC_BATCH, WINDOW, N_KV_HEADS, HEAD_DIM)
    return {
        'x_prompt': nrm(ks[0], (BATCH, SEQ, D_MODEL), 1.0),
        'x_sample': nrm(ks[1], (DEC_BATCH, DEC_SEQ, D_MODEL), 1.0),
        'cache_meta_k': nrm(ks[2], meta_shape, 1.0),
        'cache_meta_v': nrm(ks[3], meta_shape, 1.0),
        'cache_win_k': nrm(ks[4], win_shape, 1.0),
        'cache_win_v': nrm(ks[5], win_shape, 1.0),
        'state_hgrn': nrm(ks[6], (DEPTH, DEC_BATCH, HG_HEADS, HG_KDIM, HG_VDIM), 0.5),
        'state_conv': nrm(ks[7], (DEPTH, DEC_BATCH, CONV_W - 1, D_FF), 1.0),
        'meta_tokens': nrm(ks[8], (N_META, D_MODEL), 1.0),
        'w_in': nrm(ks[9], (DEPTH, D_MODEL, IN_COLS), D_MODEL ** -0.5),
        'attn_sinks': nrm(ks[10], (DEPTH, N_HEADS), 0.5),
        'hgrn_lb': nrm(ks[11], (DEPTH + 1, HG_FWIDTH), 0.1),
        'hgrn_norm_g': 1.0 + nrm(ks[12], (DEPTH, HG_VWIDTH), 0.05),
        'w_out': nrm(ks[13], (DEPTH, D_MODEL, D_MODEL), DEEPNORM_BETA * D_MODEL ** -0.5),
        'ln1_g': 1.0 + nrm(ks[14], (DEPTH, D_MODEL), 0.05),
        'ln1_b': nrm(ks[15], (DEPTH, D_MODEL), 0.02),
        'w_up': nrm(ks[16], (DEPTH, D_MODEL, 2 * D_FF), D_MODEL ** -0.5),
        'conv_w': nrm(ks[17], (DEPTH, CONV_W, D_FF), CONV_W ** -0.5),
        'conv_b': nrm(ks[18], (DEPTH, D_FF), 0.02),
        'w_down': nrm(ks[19], (DEPTH, D_FF, D_MODEL), DEEPNORM_BETA * D_FF ** -0.5),
        'ln2_g': 1.0 + nrm(ks[20], (DEPTH, D_MODEL), 0.05),
        'ln2_b': nrm(ks[21], (DEPTH, D_MODEL), 0.02),
    }


def reference(x_prompt, x_sample, cache_meta_k, cache_meta_v, cache_win_k, cache_win_v, state_hgrn, state_conv,
              meta_tokens, w_in, attn_sinks, hgrn_lb, hgrn_norm_g, w_out, ln1_g, ln1_b,
              w_up, conv_w, conv_b, w_down, ln2_g, ln2_b):
    lb_all = jnp.cumsum(jax.nn.softmax(hgrn_lb.astype(jnp.float32), axis=0), axis=0)
    B = x_prompt.shape[0]
    xp = jnp.concatenate([jnp.broadcast_to(meta_tokens.astype(x_prompt.dtype), (B, N_META, D_MODEL)), x_prompt], axis=1)
    xs = x_sample
    p_states = []
    s_states = []
    for l in range(DEPTH):
        shared = (lb_all[l], w_in[l], hgrn_norm_g[l], w_out[l], ln1_g[l], ln1_b[l],
                  w_up[l], conv_w[l], conv_b[l], w_down[l], ln2_g[l], ln2_b[l])
        xp, p_attn, p_hg, p_conv = trunk_layer(
            xp, functools.partial(swa_prompt, sinks=attn_sinks[l]), hgrn_prompt,
            jnp.zeros((B, CONV_W - 1, D_FF), xp.dtype), *shared)
        xs, s_attn, s_hg, s_conv = trunk_layer(
            xs,
            functools.partial(swa_sample, meta_k=cache_meta_k[l], meta_v=cache_meta_v[l],
                              win_k=cache_win_k[l], win_v=cache_win_v[l], sinks=attn_sinks[l]),
            functools.partial(hgrn_sample, s0=state_hgrn[l]),
            state_conv[l], *shared)
        p_states.append(p_attn + (p_hg, p_conv))
        s_states.append(s_attn + (s_hg, s_conv))
    meta_k_prompt = jnp.stack([s[0] for s in p_states])
    meta_v_prompt = jnp.stack([s[1] for s in p_states])
    win_k_prompt = jnp.stack([s[2] for s in p_states])
    win_v_prompt = jnp.stack([s[3] for s in p_states])
    hgrn_prompt_state = jnp.stack([s[4] for s in p_states])
    conv_prompt_state = jnp.stack([s[5] for s in p_states])
    win_k_sample = jnp.stack([s[0] for s in s_states])
    win_v_sample = jnp.stack([s[1] for s in s_states])
    hgrn_sample_state = jnp.stack([s[2] for s in s_states])
    conv_sample_state = jnp.stack([s[3] for s in s_states])
    y_prompt = xp[:, N_META:]
    y_sample = xs
    return (y_prompt, y_sample, meta_k_prompt, meta_v_prompt, win_k_prompt, win_v_prompt,
            hgrn_prompt_state, conv_prompt_state, win_k_sample, win_v_sample, hgrn_sample_state, conv_sample_state)
```

```python
import functools

import jax
import jax.numpy as jnp
from jax import lax
from jax.experimental import pallas as pl
from jax.experimental.pallas import tpu as pltpu

F32 = jnp.float32
BF16 = jnp.bfloat16

HG_CHUNK = 64
LN_EPS = 1e-5
RMS_EPS = 1e-6
NEG_INF = -1e30

LANES = 128
HALF = LANES // 2
VMEM_LIMIT_BYTES = 56 * 1024 * 1024

DOT_COLS = 512
FF_COLS = 256

NT_DIMS = (((1,), (1,)), ((), ()))
TN_DIMS = (((0,), (0,)), ((), ()))


def _cparams():
    return pltpu.CompilerParams(dimension_semantics=("arbitrary",), vmem_limit_bytes=VMEM_LIMIT_BYTES)


def _resident(shape):
    nd = len(shape)
    return pl.BlockSpec(shape, lambda i: (0,) * nd, pipeline_mode=pl.Buffered(1))


def _rows(shape_cols, tb, col_block=0):
    return pl.BlockSpec((tb, shape_cols), lambda i: (i, col_block))


def _inproj_kernel(x_ref, w_ref, q_ref, kv_ref, hqf_ref, hi_ref, g_ref, *, d_attn, d_kv2, d_h, q_scale):
    xb = x_ref[...].astype(BF16)

    def emit(out_ref, c0, n, scale=None):
        for c in range(0, n, DOT_COLS):
            w = min(DOT_COLS, n - c)
            z = jnp.dot(xb, w_ref[:, c0 + c:c0 + c + w], preferred_element_type=F32)
            if scale is not None:
                z = z * scale
            out_ref[:, c:c + w] = z.astype(out_ref.dtype)

    c = 0
    emit(q_ref, c, d_attn, q_scale)
    c += d_attn
    emit(kv_ref, c, d_kv2)
    c += d_kv2
    emit(hqf_ref, c, 2 * d_h)
    c += 2 * d_h
    emit(hi_ref, c, d_h)
    c += d_h
    emit(g_ref, c, 3 * d_h)


def _inproj(x, w, *, tb, act_dtype, d_attn, d_kv2, d_h, q_scale):
    t, d = x.shape
    assert t % tb == 0
    outs = (
        jax.ShapeDtypeStruct((t, d_attn), act_dtype),
        jax.ShapeDtypeStruct((t, d_kv2), F32),
        jax.ShapeDtypeStruct((t, 2 * d_h), F32),
        jax.ShapeDtypeStruct((t, d_h), act_dtype),
        jax.ShapeDtypeStruct((t, 3 * d_h), act_dtype),
    )
    return pl.pallas_call(
        functools.partial(_inproj_kernel, d_attn=d_attn, d_kv2=d_kv2, d_h=d_h, q_scale=q_scale),
        grid=(t // tb,),
        in_specs=[_rows(d, tb), _resident(w.shape)],
        out_specs=[_rows(o.shape[1], tb) for o in outs],
        out_shape=outs,
        compiler_params=_cparams(),
        name="inproj",
    )(x, w)


def _low_lanes(shape):
    return lax.broadcasted_iota(jnp.int32, shape, len(shape) - 1) < HALF


def _dup_halves(tile):
    rolled = pltpu.roll(tile, HALF, axis=1)
    lo = _low_lanes(tile.shape)
    return jnp.where(lo, tile, rolled), jnp.where(lo, rolled, tile)


def _sink_col(sinks_ref, head0, n):
    r = lax.broadcasted_iota(jnp.int32, (4 * n, 1), 0)
    col = jnp.full((4 * n, 1), sinks_ref[head0 + 3], F32)
    for j in (2, 1, 0):
        col = jnp.where(r < (j + 1) * n, sinks_ref[head0 + j], col)
    return col


def _attn_core(q0, q1, kb, vb, mask, sink_col):
    n = q0.shape[0]
    lo = _low_lanes((n, LANES))
    zero = jnp.zeros((n, LANES), F32)
    qst = jnp.concatenate(
        [jnp.where(lo, q0, zero), jnp.where(lo, zero, q0), jnp.where(lo, q1, zero), jnp.where(lo, zero, q1)],
        axis=0).astype(BF16)
    s = lax.dot_general(qst, kb, NT_DIMS, preferred_element_type=F32)
    s = jnp.where(mask, s, NEG_INF)
    m = jnp.maximum(jnp.max(s, axis=1, keepdims=True), sink_col)
    p = jnp.exp(s - m)
    denom = jnp.sum(p, axis=1, keepdims=True) + jnp.exp(sink_col - m)
    pv = jnp.dot(p.astype(BF16), vb, preferred_element_type=F32) / denom
    o0 = jnp.where(lo, pv[0:n], pv[n:2 * n])
    o1 = jnp.where(lo, pv[2 * n:3 * n], pv[3 * n:4 * n])
    return o0, o1


def _gate_store(o_ref, ga_ref, rows, kvh, o0, o1):
    for j, o in enumerate((o0, o1)):
        cols = slice((2 * kvh + j) * LANES, (2 * kvh + j + 1) * LANES)
        ga = ga_ref[rows, cols].astype(F32)
        o_ref[rows, cols] = (jax.nn.sigmoid(ga) * o).astype(o_ref.dtype)


def _attn_meta_kernel(sinks_ref, q_ref, kv_ref, ga_ref, o_ref, *, n_kv):
    n = q_ref.shape[0]
    d_kv = n_kv * HALF
    r = lax.broadcasted_iota(jnp.int32, (4 * n, n), 0) & (n - 1)
    c = lax.broadcasted_iota(jnp.int32, (4 * n, n), 1)
    mask = c <= r
    rows = slice(0, n)
    for pair in range(n_kv // 2):
        k2 = _dup_halves(kv_ref[:, pair * LANES:(pair + 1) * LANES])
        v2 = _dup_halves(kv_ref[:, d_kv + pair * LANES:d_kv + (pair + 1) * LANES])
        for par in range(2):
            kvh = 2 * pair + par
            q0 = q_ref[:, (2 * kvh) * LANES:(2 * kvh + 1) * LANES].astype(F32)
            q1 = q_ref[:, (2 * kvh + 1) * LANES:(2 * kvh + 2) * LANES].astype(F32)
            o0, o1 = _attn_core(q0, q1, k2[par].astype(BF16), v2[par].astype(BF16), mask,
                                _sink_col(sinks_ref, 4 * kvh, n))
            _gate_store(o_ref, ga_ref, rows, kvh, o0, o1)


def _attn_prompt_kernel(sinks_ref, q_ref, kv_ref, ga_ref, mkv_ref, o_ref, kbuf, vbuf, *, n_kv, n_meta):
    b = pl.program_id(0)
    w = q_ref.shape[0]
    d_kv = n_kv * HALF
    prev0, cur0, nk = n_meta, n_meta + w, n_meta + 2 * w

    @pl.when(b == 0)
    def _():
        for pair in range(n_kv // 2):
            k2 = _dup_halves(mkv_ref[:, pair * LANES:(pair + 1) * LANES])
            v2 = _dup_halves(mkv_ref[:, d_kv + pair * LANES:d_kv + (pair + 1) * LANES])
            for par in range(2):
                kvh = 2 * pair + par
                kbuf[kvh, 0:n_meta, :] = k2[par].astype(BF16)
                vbuf[kvh, 0:n_meta, :] = v2[par].astype(BF16)
                kbuf[kvh, prev0:cur0, :] = jnp.zeros((w, LANES), BF16)
                vbuf[kvh, prev0:cur0, :] = jnp.zeros((w, LANES), BF16)

    for pair in range(n_kv // 2):
        k2 = _dup_halves(kv_ref[:, pair * LANES:(pair + 1) * LANES])
        v2 = _dup_halves(kv_ref[:, d_kv + pair * LANES:d_kv + (pair + 1) * LANES])
        for par in range(2):
            kbuf[2 * pair + par, cur0:nk, :] = k2[par].astype(BF16)
            vbuf[2 * pair + par, cur0:nk, :] = v2[par].astype(BF16)

    t = lax.broadcasted_iota(jnp.int32, (4 * w, nk), 0) & (w - 1)
    c = lax.broadcasted_iota(jnp.int32, (4 * w, nk), 1)
    rel = c - cur0
    first = jnp.where(b == 0, 0, -w)
    mask = (c < n_meta) | ((rel <= t) & (rel > t - w) & (rel >= first))

    rows = slice(0, w)
    for kvh in range(n_kv):
        q0 = q_ref[:, (2 * kvh) * LANES:(2 * kvh + 1) * LANES].astype(F32)
        q1 = q_ref[:, (2 * kvh + 1) * LANES:(2 * kvh + 2) * LANES].astype(F32)
        o0, o1 = _attn_core(q0, q1, kbuf[kvh], vbuf[kvh], mask, _sink_col(sinks_ref, 4 * kvh, w))
        _gate_store(o_ref, ga_ref, rows, kvh, o0, o1)

    for kvh in range(n_kv):
        kbuf[kvh, prev0:cur0, :] = kbuf[kvh, cur0:nk, :]
        vbuf[kvh, prev0:cur0, :] = vbuf[kvh, cur0:nk, :]


def _attn_sample_kernel(sinks_ref, q_ref, kv_ref, ga_ref, mk_ref, mv_ref, wk_ref, wv_ref,
                        o_ref, wko_ref, wvo_ref, *, n_kv):
    n_seq, win, d_kv = wk_ref.shape
    n_meta = mk_ref.shape[1]
    t_new = q_ref.shape[0] // n_seq
    pad = (-(win + n_meta + t_new)) % 16
    nk = win + n_meta + t_new + pad

    j = lax.broadcasted_iota(jnp.int32, (4 * t_new, nk), 0) & (t_new - 1)
    c = lax.broadcasted_iota(jnp.int32, (4 * t_new, nk), 1)
    new0 = win + n_meta
    mask = ((c < win) & (c > j)) | ((c >= win) & (c < new0)) | ((c >= new0) & (c - new0 <= j))
    zpad = jnp.zeros((pad, LANES), F32)

    def seq_body(s, carry):
        rows = pl.ds(pl.multiple_of(s * t_new, t_new), t_new)
        knew = kv_ref[rows, 0:d_kv]
        vnew = kv_ref[rows, d_kv:2 * d_kv]
        wk = wk_ref[s]
        wv = wv_ref[s]
        wko_ref[s, 0:win - t_new, :] = wk[t_new:win]
        wko_ref[s, win - t_new:win, :] = knew
        wvo_ref[s, 0:win - t_new, :] = wv[t_new:win]
        wvo_ref[s, win - t_new:win, :] = vnew
        mk = mk_ref[s]
        mv = mv_ref[s]
        for pair in range(n_kv // 2):
            cols = slice(pair * LANES, (pair + 1) * LANES)
            pieces_k = [wk[:, cols], mk[:, cols], knew[:, cols]]
            pieces_v = [wv[:, cols], mv[:, cols], vnew[:, cols]]
            if pad:
                pieces_k.append(zpad)
                pieces_v.append(zpad)
            k2 = _dup_halves(jnp.concatenate(pieces_k, axis=0))
            v2 = _dup_halves(jnp.concatenate(pieces_v, axis=0))
            for par in range(2):
                kvh = 2 * pair + par
                q0 = q_ref[rows, (2 * kvh) * LANES:(2 * kvh + 1) * LANES].astype(F32)
                q1 = q_ref[rows, (2 * kvh + 1) * LANES:(2 * kvh + 2) * LANES].astype(F32)
                o0, o1 = _attn_core(q0, q1, k2[par].astype(BF16), v2[par].astype(BF16), mask,
                                    _sink_col(sinks_ref, 4 * kvh, t_new))
                _gate_store(o_ref, ga_ref, rows, kvh, o0, o1)
        return carry

    lax.fori_loop(0, n_seq, seq_body, 0)


def _smem_spec():
    return pl.BlockSpec(memory_space=pltpu.SMEM)


def _attn_meta(sinks, q, kv, g, *, n_kv):
    n, d = q.shape
    full = lambda cols: pl.BlockSpec((n, cols), lambda i: (0, 0))
    return pl.pallas_call(
        functools.partial(_attn_meta_kernel, n_kv=n_kv),
        grid=(1,),
        in_specs=[_smem_spec(), full(d), full(kv.shape[1]), pl.BlockSpec((n, d), lambda i: (0, 1))],
        out_specs=full(d),
        out_shape=jax.ShapeDtypeStruct((n, d), q.dtype),
        compiler_params=_cparams(),
        name="attn_meta",
    )(sinks, q, kv, g)


def _attn_prompt(sinks, q, kv, g, mkv, *, n_kv, window):
    t, d = q.shape
    n_meta = mkv.shape[0]
    nk = n_meta + 2 * window
    return pl.pallas_call(
        functools.partial(_attn_prompt_kernel, n_kv=n_kv, n_meta=n_meta),
        grid=(t // window,),
        in_specs=[_smem_spec(), _rows(d, window), _rows(kv.shape[1], window), _rows(d, window, 1),
                  pl.BlockSpec(mkv.shape, lambda i: (0, 0))],
        out_specs=_rows(d, window),
        out_shape=jax.ShapeDtypeStruct((t, d), q.dtype),
        scratch_shapes=[pltpu.VMEM((n_kv, nk, LANES), BF16), pltpu.VMEM((n_kv, nk, LANES), BF16)],
        compiler_params=_cparams(),
        name="attn_prompt",
    )(sinks, q, kv, g, mkv)


def _attn_sample(sinks, q, kv, g, mk, mv, wk, wv, *, n_kv, seq_block):
    t, d = q.shape
    n_seq, win, d_kv = wk.shape
    t_new = t // n_seq
    tb = seq_block * t_new
    cache = lambda a: pl.BlockSpec((seq_block,) + a.shape[1:], lambda i: (i, 0, 0))
    return pl.pallas_call(
        functools.partial(_attn_sample_kernel, n_kv=n_kv),
        grid=(n_seq // seq_block,),
        in_specs=[_smem_spec(), _rows(d, tb), _rows(kv.shape[1], tb), _rows(d, tb, 1),
                  cache(mk), cache(mv), cache(wk), cache(wv)],
        out_specs=[_rows(d, tb), cache(wk), cache(wv)],
        out_shape=(jax.ShapeDtypeStruct((t, d), q.dtype), jax.ShapeDtypeStruct(wk.shape, F32),
                   jax.ShapeDtypeStruct(wv.shape, F32)),
        compiler_params=_cparams(),
        name="attn_sample",
    )(sinks, q, kv, g, mk, mv, wk, wv)


def _layer_lower_bound(lb_ref, layer):
    r = lb_ref[...]
    e = jnp.exp(r - jnp.max(r, axis=0, keepdims=True))
    return jnp.sum(e[0:layer + 1], axis=0, keepdims=True) / jnp.sum(e, axis=0, keepdims=True)


def _chunk_cumsum(x, chunk):
    r = lax.broadcasted_iota(jnp.int32, x.shape, 0) & (chunk - 1)
    s = 1
    while s < chunk:
        x = x + jnp.where(r >= s, pltpu.roll(x, s, axis=0), 0.0)
        s *= 2
    return x


def _hgrn_gates(hq, hf, lb, chunk):
    f = lb + (1.0 - lb) * jax.nn.sigmoid(hf)
    kk = 1.0 - f
    qq = hq * jax.nn.sigmoid(hq)
    b = _chunk_cumsum(jnp.log(f), chunk)
    return qq * jnp.exp(b), kk * jnp.exp(-b), kk, b


def _causal(chunk):
    r = lax.broadcasted_iota(jnp.int32, (chunk, chunk), 0)
    c = lax.broadcasted_iota(jnp.int32, (chunk, chunk), 1)
    return c <= r


def _merge_store(m_ref, o, ng_ref, hg_ref, gb_ref, a_ref):
    on = o * lax.rsqrt(jnp.mean(o * o, axis=-1, keepdims=True) + RMS_EPS) * ng_ref[...]
    hg = hg_ref[...].astype(F32)
    gated = on * (hg * jax.nn.sigmoid(hg))
    merged = a_ref[...].astype(F32) + jax.nn.sigmoid(gb_ref[...].astype(F32)) * gated
    m_ref[...] = merged.astype(m_ref.dtype)


def _hgrn_seq_kernel(lb_ref, ng_ref, hqf_ref, hi_ref, hg_ref, gb_ref, a_ref, sinit_ref,
                     m_ref, sout_ref, st_scr, o_scr, *, chunk, layer, n_heads):
    i = pl.program_id(0)
    rb, d_h = hi_ref.shape
    dk = d_h // n_heads

    @pl.when(i == 0)
    def _():
        st_scr[...] = sinit_ref[...]

    lb = _layer_lower_bound(lb_ref, layer)
    causal = _causal(chunk)

    def chunk_body(ci, carry):
        rows = pl.ds(pl.multiple_of(ci * chunk, chunk), chunk)
        qd, kd, kk, b = _hgrn_gates(hqf_ref[rows, 0:d_h], hqf_ref[rows, d_h:2 * d_h], lb, chunk)
        b_last = b[chunk - 1:chunk, :]
        ke = kk * jnp.exp(b_last - b)
        decay = jnp.exp(b_last)
        v = hi_ref[rows, :].astype(BF16)
        for h in range(n_heads):
            sl = slice(h * dk, (h + 1) * dk)
            qd_h = qd[:, sl].astype(BF16)
            kd_h = kd[:, sl].astype(BF16)
            ke_h = ke[:, sl].astype(BF16)
            v_h = v[:, sl]
            a = lax.dot_general(qd_h, kd_h, NT_DIMS, preferred_element_type=F32)
            a = jnp.where(causal, a, 0.0).astype(BF16)
            st = st_scr[h]
            o = jnp.dot(a, v_h, preferred_element_type=F32)
            o = o + lax.dot_general(qd_h, st.astype(BF16), NT_DIMS, preferred_element_type=F32)
            o_scr[rows, sl] = o
            st_scr[h] = st * decay[:, sl] + lax.dot_general(v_h, ke_h, TN_DIMS, preferred_element_type=F32)
        return carry

    lax.fori_loop(0, rb // chunk, chunk_body, 0)
    _merge_store(m_ref, o_scr[...], ng_ref, hg_ref, gb_ref, a_ref)
    sout_ref[...] = st_scr[...]


def _hgrn_sample_kernel(lb_ref, ng_ref, hqf_ref, hi_ref, hg_ref, gb_ref, a_ref, s0_ref,
                        m_ref, sout_ref, qd_scr, kd_scr, ke_scr, dec_scr, o_scr, *, layer):
    n_seq, n_heads, dk, dv = s0_ref.shape
    rb, d_h = hi_ref.shape
    t_new = rb // n_seq
    lb = _layer_lower_bound(lb_ref, layer)
    qd, kd, kk, b = _hgrn_gates(hqf_ref[:, 0:d_h], hqf_ref[:, d_h:2 * d_h], lb, t_new)
    r = lax.broadcasted_iota(jnp.int32, b.shape, 0) & (t_new - 1)
    b_last = b
    s = 1
    while s < t_new:
        b_last = jnp.where(r + s < t_new, pltpu.roll(b_last, rb - s, axis=0), b_last)
        s *= 2
    qd_scr[...] = qd
    kd_scr[...] = kd
    ke_scr[...] = kk * jnp.exp(b_last - b)
    dec_scr[...] = jnp.exp(b_last)
    causal = _causal(t_new)

    def seq_body(si, carry):
        rows = pl.ds(pl.multiple_of(si * t_new, t_new), t_new)
        qd_s = qd_scr[rows, :]
        kd_s = kd_scr[rows, :]
        ke_s = ke_scr[rows, :]
        dec_s = dec_scr[rows, :]
        v_s = hi_ref[rows, :]
        for h in range(n_heads):
            sl = slice(h * dk, (h + 1) * dk)
            qd_h = qd_s[:, sl].astype(BF16)
            kd_h = kd_s[:, sl].astype(BF16)
            ke_h = ke_s[:, sl].astype(BF16)
            v_h = v_s[:, sl].astype(BF16)
            a = lax.dot_general(qd_h, kd_h, NT_DIMS, preferred_element_type=F32)
            a = jnp.where(causal, a, 0.0).astype(BF16)
            s0 = s0_ref[si, h]
            o = jnp.dot(a, v_h, preferred_element_type=F32)
            o = o + jnp.dot(qd_h, s0.astype(BF16), preferred_element_type=F32)
            o_scr[rows, sl] = o
            decay_t = jnp.broadcast_to(dec_s[0:1, sl], (dk, dk)).T
            sout_ref[si, h] = s0 * decay_t + lax.dot_general(ke_h, v_h, TN_DIMS, preferred_element_type=F32)
        return carry

    lax.fori_loop(0, n_seq, seq_body, 0)
    _merge_store(m_ref, o_scr[...], ng_ref, hg_ref, gb_ref, a_ref)


def _hgrn_seq(lbp, ng, hqf, hi, g, a, sinit, *, rb, chunk, layer):
    t, d_h = hi.shape
    n_heads = sinit.shape[0]
    state = pl.BlockSpec(sinit.shape, lambda i: (0, 0, 0))
    return pl.pallas_call(
        functools.partial(_hgrn_seq_kernel, chunk=chunk, layer=layer, n_heads=n_heads),
        grid=(t // rb,),
        in_specs=[pl.BlockSpec(lbp.shape, lambda i: (0, 0)), pl.BlockSpec(ng.shape, lambda i: (0, 0)),
                  _rows(2 * d_h, rb), _rows(d_h, rb), _rows(d_h, rb, 0), _rows(d_h, rb, 2), _rows(d_h, rb), state],
        out_specs=[_rows(d_h, rb), state],
        out_shape=(jax.ShapeDtypeStruct((t, d_h), a.dtype), jax.ShapeDtypeStruct(sinit.shape, F32)),
        scratch_shapes=[pltpu.VMEM(sinit.shape, F32), pltpu.VMEM((rb, d_h), F32)],
        compiler_params=_cparams(),
        name="hgrn_seq",
    )(lbp, ng, hqf, hi, g, g, a, sinit)


def _hgrn_sample(lbp, ng, hqf, hi, g, a, s0, *, seq_block, layer):
    t, d_h = hi.shape
    n_seq = s0.shape[0]
    rb = seq_block * (t // n_seq)
    state = pl.BlockSpec((seq_block,) + s0.shape[1:], lambda i: (i, 0, 0, 0))
    return pl.pallas_call(
        functools.partial(_hgrn_sample_kernel, layer=layer),
        grid=(n_seq // seq_block,),
        in_specs=[pl.BlockSpec(lbp.shape, lambda i: (0, 0)), pl.BlockSpec(ng.shape, lambda i: (0, 0)),
                  _rows(2 * d_h, rb), _rows(d_h, rb), _rows(d_h, rb, 0), _rows(d_h, rb, 2), _rows(d_h, rb), state],
        out_specs=[_rows(d_h, rb), state],
        out_shape=(jax.ShapeDtypeStruct((t, d_h), a.dtype), jax.ShapeDtypeStruct(s0.shape, F32)),
        scratch_shapes=[pltpu.VMEM((rb, d_h), F32)] * 5,
        compiler_params=_cparams(),
        name="hgrn_sample",
    )(lbp, ng, hqf, hi, g, g, a, s0)


def _layer_norm(v, g_ref, b_ref):
    mu = jnp.mean(v, axis=-1, keepdims=True)
    c = v - mu
    var = jnp.mean(c * c, axis=-1, keepdims=True)
    return c * lax.rsqrt(var + LN_EPS) * g_ref[...] + b_ref[...]


def _outffn_kernel(x_ref, m_ref, hist_ref, wo_ref, g1_ref, b1_ref, wu_ref, cw_ref, cb_ref, wd_ref,
                   g2_ref, b2_ref, y_ref, tail_ref, ubuf, hist_scr, *, alpha, shift):
    i = pl.program_id(0)
    tb = x_ref.shape[0]
    d_ff = wd_ref.shape[0]
    nh = hist_ref.shape[0]
    hpad = hist_scr.shape[0]

    @pl.when(i == 0)
    def _():
        hist_scr[hpad - nh:hpad, :] = hist_ref[...]

    x = x_ref[...]
    mix = jnp.dot(m_ref[...].astype(BF16), wo_ref[...], preferred_element_type=F32)
    y1 = _layer_norm(alpha * x + mix, g1_ref, b1_ref)
    y1b = y1.astype(BF16)

    acc = jnp.zeros((tb, y_ref.shape[1]), F32)
    for c0 in range(0, d_ff, FF_COLS):
        cols = slice(c0, c0 + FF_COLS)
        u = jnp.dot(y1b, wu_ref[:, cols], preferred_element_type=F32)
        val = jnp.dot(y1b, wu_ref[:, d_ff + c0:d_ff + c0 + FF_COLS], preferred_element_type=F32)
        ubuf[hpad - nh:hpad, :] = hist_scr[hpad - nh:hpad, cols]
        ubuf[hpad:hpad + tb, :] = u
        u1 = ubuf[hpad - shift:hpad - shift + tb, :]
        u2 = ubuf[hpad - 2 * shift:hpad - 2 * shift + tb, :]
        conv = cb_ref[:, cols] + cw_ref[0:1, cols] * u2 + cw_ref[1:2, cols] * u1 + cw_ref[2:3, cols] * u
        h = (jax.nn.gelu(conv) * val).astype(BF16)
        acc = acc + jnp.dot(h, wd_ref[cols, :], preferred_element_type=F32)
        hist_scr[hpad - nh:hpad, cols] = ubuf[hpad + tb - nh:hpad + tb, :]

    y_ref[...] = _layer_norm(alpha * y1 + acc, g2_ref, b2_ref)
    tail_ref[...] = hist_scr[hpad - nh:hpad, :]


def _outffn(x, m, hist, wo, g1, b1, wu, cw, cb, wd, g2, b2, *, tb, alpha, shift):
    t, d = x.shape
    d_ff = wd.shape[0]
    assert t % tb == 0 and d_ff % FF_COLS == 0 and hist.shape == (2 * shift, d_ff)
    hpad = -(-2 * shift // 8) * 8
    small = lambda a: pl.BlockSpec(a.shape, lambda i: (0, 0))
    return pl.pallas_call(
        functools.partial(_outffn_kernel, alpha=alpha, shift=shift),
        grid=(t // tb,),
        in_specs=[_rows(d, tb), _rows(d, tb), small(hist), _resident(wo.shape), small(g1), small(b1),
                  _resident(wu.shape), small(cw), small(cb), _resident(wd.shape), small(g2), small(b2)],
        out_specs=[_rows(d, tb), small(hist)],
        out_shape=(jax.ShapeDtypeStruct((t, d), F32), jax.ShapeDtypeStruct(hist.shape, F32)),
        scratch_shapes=[pltpu.VMEM((hpad + tb, FF_COLS), F32), pltpu.VMEM((hpad, d_ff), F32)],
        compiler_params=_cparams(),
        name="outffn",
    )(x, m, hist, wo, g1, b1, wu, cw, cb, wd, g2, b2)


def kernel(x_prompt, x_sample, cache_meta_k, cache_meta_v, cache_win_k, cache_win_v, state_hgrn, state_conv,
           meta_tokens, w_in, attn_sinks, hgrn_lb, hgrn_norm_g, w_out, ln1_g, ln1_b,
           w_up, conv_w, conv_b, w_down, ln2_g, ln2_b):
    depth = w_in.shape[0]
    assert depth == 1, "only a single layer is supported"
    layer = 0
    batch, seq, d = x_prompt.shape
    assert batch == 1
    n_seq, t_new, _ = x_sample.shape
    _, _, n_meta, n_kv, head_dim = cache_meta_k.shape
    window = cache_win_k.shape[2]
    n_heads = attn_sinks.shape[1]
    _, _, hg_heads, hg_k, hg_v = state_hgrn.shape
    d_ff = w_down.shape[1]
    d_attn = n_heads * head_dim
    d_kv = n_kv * head_dim
    d_h = hg_heads * hg_k
    assert head_dim == HALF and n_heads == 4 * n_kv and n_kv % 2 == 0 and hg_k == hg_v == LANES
    assert d_attn == d and d_h == d and window == LANES and seq % 512 == 0
    alpha = (2.0 * depth) ** 0.25

    wi = w_in[layer].astype(BF16)
    wo = w_out[layer].astype(BF16)
    wu = w_up[layer].astype(BF16)
    wd = w_down[layer].astype(BF16)
    sinks = attn_sinks[layer].astype(F32)
    ng = hgrn_norm_g[layer].reshape(1, d_h)
    g1, b1 = ln1_g[layer].reshape(1, d), ln1_b[layer].reshape(1, d)
    g2, b2 = ln2_g[layer].reshape(1, d), ln2_b[layer].reshape(1, d)
    cw, cb = conv_w[layer], conv_b[layer].reshape(1, d_ff)

    inproj = functools.partial(_inproj, w=wi, d_attn=d_attn, d_kv2=2 * d_kv, d_h=d_h, q_scale=head_dim ** -0.5)
    outffn = functools.partial(_outffn, wo=wo, g1=g1, b1=b1, wu=wu, cw=cw, cb=cb, wd=wd, g2=g2, b2=b2, alpha=alpha)

    xm = meta_tokens.astype(F32)
    q_m, kv_m, hqf_m, hi_m, g_m = inproj(xm, tb=n_meta, act_dtype=BF16)
    a_m = _attn_meta(sinks, q_m, kv_m, g_m, n_kv=n_kv)
    merged_m, st_m = _hgrn_seq(hgrn_lb, ng, hqf_m, hi_m, g_m, a_m, jnp.zeros((hg_heads, hg_v, hg_k), F32),
                               rb=n_meta, chunk=n_meta, layer=layer)
    _, tail_m = outffn(xm, merged_m, jnp.zeros((2, d_ff), F32), tb=n_meta, shift=1)

    xp = x_prompt[0]
    q_p, kv_p, hqf_p, hi_p, g_p = inproj(xp, tb=512, act_dtype=BF16)
    a_p = _attn_prompt(sinks, q_p, kv_p, g_p, kv_m, n_kv=n_kv, window=window)
    merged_p, st_p = _hgrn_seq(hgrn_lb, ng, hqf_p, hi_p, g_p, a_p, st_m, rb=256, chunk=HG_CHUNK, layer=layer)
    y_p, tail_p = outffn(xp, merged_p, tail_m, tb=512, shift=1)

    xs = x_sample.reshape(n_seq * t_new, d)
    tb_s = min(512, n_seq * t_new)
    q_s, kv_s, hqf_s, hi_s, g_s = inproj(xs, tb=tb_s, act_dtype=F32)
    a_s, wk_s, wv_s = _attn_sample(
        sinks, q_s, kv_s, g_s,
        cache_meta_k[layer].reshape(n_seq, n_meta, d_kv), cache_meta_v[layer].reshape(n_seq, n_meta, d_kv),
        cache_win_k[layer].reshape(n_seq, window, d_kv), cache_win_v[layer].reshape(n_seq, window, d_kv),
        n_kv=n_kv, seq_block=8)
    merged_s, st_s = _hgrn_sample(hgrn_lb, ng, hqf_s, hi_s, g_s, a_s, state_hgrn[layer], seq_block=8, layer=layer)
    time_major = lambda a: a.reshape(n_seq, t_new, d).transpose(1, 0, 2).reshape(t_new * n_seq, d)
    hist_s = state_conv[layer].transpose(1, 0, 2).reshape(2 * n_seq, d_ff)
    y_s, tail_s = outffn(time_major(xs), time_major(merged_s), hist_s, tb=tb_s, shift=n_seq)

    kv_heads = lambda a, rows: a.reshape(1, -1, rows, n_kv, head_dim)
    return (
        y_p[None],
        y_s.reshape(t_new, n_seq, d).transpose(1, 0, 2),
        kv_heads(kv_m[:, :d_kv], n_meta),
        kv_heads(kv_m[:, d_kv:], n_meta),
        kv_heads(kv_p[seq - window:, :d_kv], window),
        kv_heads(kv_p[seq - window:, d_kv:], window),
        st_p.transpose(0, 2, 1)[None, None],
        tail_p[None, None],
        kv_heads(wk_s, window),
        kv_heads(wv_s, window),
        st_s[None],
        tail_s.reshape(2, n_seq, d_ff).transpose(1, 0, 2)[None],
    )
```

```python
import functools
import math

import jax
import jax.numpy as jnp
from jax import lax
from jax.experimental import pallas as pl
from jax.experimental.pallas import tpu as pltpu

F32 = jnp.float32
BF16 = jnp.bfloat16

HG_CHUNK = 64
LN_EPS = 1e-5
RMS_EPS = 1e-6
NEG_INF = -1e30
LOG2E = math.log2(math.e)

LANES = 128
HALF = LANES // 2
VMEM_LIMIT_BYTES = 56 * 1024 * 1024

DOT_COLS = 512
FF_COLS = 256
ROW_BLOCK = 512

NT_DIMS = (((1,), (1,)), ((), ()))
TN_DIMS = (((0,), (0,)), ((), ()))


def _cparams():
    return pltpu.CompilerParams(dimension_semantics=("arbitrary",), vmem_limit_bytes=VMEM_LIMIT_BYTES)


def _resident(shape):
    nd = len(shape)
    return pl.BlockSpec(shape, lambda i: (0,) * nd, pipeline_mode=pl.Buffered(1))


def _small(a):
    nd = a.ndim
    return pl.BlockSpec(a.shape, lambda i: (0,) * nd)


def _rows(cols, tb):
    return pl.BlockSpec((tb, cols), lambda i: (i, 0))


def _sigmoid(x):
    return 0.5 * jnp.tanh(0.5 * x) + 0.5


def _layer_lower_bound(lb_ref, layer):
    r = lb_ref[...]
    e = jnp.exp(r - jnp.max(r, axis=0, keepdims=True))
    return jnp.sum(e[0:layer + 1], axis=0, keepdims=True) / jnp.sum(e, axis=0, keepdims=True)


def _chunk_cumprod(x, chunk):
    r = lax.broadcasted_iota(jnp.int32, x.shape, 0) & (chunk - 1)
    s = 1
    while s < chunk:
        x = x * jnp.where(r >= s, pltpu.roll(x, s, axis=0), 1.0)
        s *= 2
    return x


def _inproj_kernel(lb_ref, x_ref, w_ref, q_ref, kv_ref, qd_ref, kd_ref, dec_ref, hi_ref, sga_ref, gg_ref, eb_scr,
                   *, d_attn, d_kv2, d_h, q_scale, chunk, layer):
    xb = x_ref[...].astype(BF16)
    tb = x_ref.shape[0]

    def mm(c0, n):
        return jnp.dot(xb, w_ref[:, c0:c0 + n], preferred_element_type=F32)

    off_kv = d_attn
    off_hq = off_kv + d_kv2
    off_hf, off_hi, off_hg, off_ga, off_gb = (off_hq + j * d_h for j in range(1, 6))

    for c in range(0, d_attn, DOT_COLS):
        q_ref[:, c:c + DOT_COLS] = (mm(c, DOT_COLS) * q_scale).astype(q_ref.dtype)
    kv_ref[...] = mm(off_kv, d_kv2)

    lb = _layer_lower_bound(lb_ref, layer)
    for c in range(0, d_h, DOT_COLS):
        cs = slice(c, c + DOT_COLS)
        hq = mm(off_hq + c, DOT_COLS)
        f = lb[:, cs] + (1.0 - lb[:, cs]) * _sigmoid(mm(off_hf + c, DOT_COLS))
        eb = _chunk_cumprod(f, chunk)
        qd_ref[:, cs] = (hq * _sigmoid(hq) * eb).astype(qd_ref.dtype)
        kd_ref[:, cs] = ((1.0 - f) / eb).astype(kd_ref.dtype)
        for j in range(DOT_COLS // LANES):
            tile = (c // LANES) + j
            eb_scr[tile] = eb[:, j * LANES:(j + 1) * LANES]
            dec_ref[:, tile * LANES:(tile + 1) * LANES] = eb_scr[tile, pl.ds(chunk - 1, tb // chunk, stride=chunk), :]
        hi_ref[:, cs] = mm(off_hi + c, DOT_COLS).astype(hi_ref.dtype)
        sga_ref[:, cs] = _sigmoid(mm(off_ga + c, DOT_COLS)).astype(sga_ref.dtype)
        hg = mm(off_hg + c, DOT_COLS)
        gg_ref[:, cs] = (_sigmoid(mm(off_gb + c, DOT_COLS)) * hg * _sigmoid(hg)).astype(gg_ref.dtype)


def _inproj(lbp, x, w, *, tb, chunk, act_dtype, d_attn, d_kv2, d_h, q_scale, layer):
    t, d = x.shape
    assert t % tb == 0 and tb % chunk == 0 and d_attn % DOT_COLS == 0 and d_h % DOT_COLS == 0
    act = lambda cols: jax.ShapeDtypeStruct((t, cols), act_dtype)
    outs = (act(d_attn), jax.ShapeDtypeStruct((t, d_kv2), F32), act(d_h), act(d_h),
            jax.ShapeDtypeStruct((t // chunk, d_h), F32), act(d_h), act(d_h), act(d_h))
    out_specs = [_rows(o.shape[1], tb) for o in outs]
    out_specs[4] = _rows(d_h, tb // chunk)
    return pl.pallas_call(
        functools.partial(_inproj_kernel, d_attn=d_attn, d_kv2=d_kv2, d_h=d_h, q_scale=q_scale, chunk=chunk,
                          layer=layer),
        grid=(t // tb,),
        in_specs=[_small(lbp), _rows(d, tb), _resident(w.shape)],
        out_specs=out_specs,
        out_shape=outs,
        scratch_shapes=[pltpu.VMEM((d_h // LANES, tb, LANES), F32)],
        compiler_params=_cparams(),
        name="inproj",
    )(lbp, x, w)


def _low_lanes(shape):
    return lax.broadcasted_iota(jnp.int32, shape, len(shape) - 1) < HALF


def _dup_halves(tile):
    rolled = pltpu.roll(tile, HALF, axis=1)
    lo = _low_lanes(tile.shape)
    return jnp.where(lo, tile, rolled), jnp.where(lo, rolled, tile)


def _sink_col(sinks_ref, head0, n):
    r = lax.broadcasted_iota(jnp.int32, (4 * n, 1), 0)
    col = jnp.full((4 * n, 1), sinks_ref[head0 + 3] * LOG2E, F32)
    for j in (2, 1, 0):
        col = jnp.where(r < (j + 1) * n, sinks_ref[head0 + j] * LOG2E, col)
    return col


def _stack_heads(q0, q1):
    lo = _low_lanes(q0.shape)
    zero = jnp.zeros(q0.shape, F32)
    return jnp.concatenate(
        [jnp.where(lo, q0, zero), jnp.where(lo, zero, q0), jnp.where(lo, q1, zero), jnp.where(lo, zero, q1)],
        axis=0).astype(BF16)


def _unstack_heads(pv, n):
    lo = _low_lanes((n, LANES))
    return jnp.where(lo, pv[0:n], pv[n:2 * n]), jnp.where(lo, pv[2 * n:3 * n], pv[3 * n:4 * n])


def _attn_core(q0, q1, kb, vb, mask, sink_col):
    n = q0.shape[0]
    s = lax.dot_general(_stack_heads(q0, q1), kb, NT_DIMS, preferred_element_type=F32)
    s = jnp.where(mask, s, NEG_INF)
    m = jnp.maximum(jnp.max(s, axis=1, keepdims=True), sink_col)
    p = jnp.exp2(s - m)
    denom = jnp.sum(p, axis=1, keepdims=True) + jnp.exp2(sink_col - m)
    pv = jnp.dot(p.astype(BF16), vb, preferred_element_type=F32) / denom
    return _unstack_heads(pv, n)


def _gate_store(o_ref, sga_ref, rows, kvh, o0, o1):
    for j, o in enumerate((o0, o1)):
        cols = slice((2 * kvh + j) * LANES, (2 * kvh + j + 1) * LANES)
        o_ref[rows, cols] = (sga_ref[rows, cols].astype(F32) * o).astype(o_ref.dtype)


def _q_tiles(q_ref, rows, kvh):
    return (q_ref[rows, (2 * kvh) * LANES:(2 * kvh + 1) * LANES].astype(F32),
            q_ref[rows, (2 * kvh + 1) * LANES:(2 * kvh + 2) * LANES].astype(F32))


def _attn_meta_kernel(sinks_ref, q_ref, kv_ref, sga_ref, o_ref, *, n_kv):
    n = q_ref.shape[0]
    d_kv = n_kv * HALF
    r = lax.broadcasted_iota(jnp.int32, (4 * n, n), 0) & (n - 1)
    c = lax.broadcasted_iota(jnp.int32, (4 * n, n), 1)
    mask = c <= r
    rows = slice(0, n)
    for pair in range(n_kv // 2):
        k2 = _dup_halves(kv_ref[:, pair * LANES:(pair + 1) * LANES])
        v2 = _dup_halves(kv_ref[:, d_kv + pair * LANES:d_kv + (pair + 1) * LANES])
        for par in range(2):
            kvh = 2 * pair + par
            q0, q1 = _q_tiles(q_ref, rows, kvh)
            o0, o1 = _attn_core(q0, q1, k2[par].astype(BF16), v2[par].astype(BF16), mask,
                                _sink_col(sinks_ref, 4 * kvh, n))
            _gate_store(o_ref, sga_ref, rows, kvh, o0, o1)


def _attn_prompt_kernel(sinks_ref, q_ref, kv_ref, sga_ref, mkv_ref, o_ref, kbuf, vbuf, *, n_kv, n_meta, window):
    b = pl.program_id(0)
    w = window
    d_kv = n_kv * HALF
    meta0 = 2 * w
    assert q_ref.shape[0] == 2 * w

    @pl.when(b == 0)
    def _():
        for pair in range(n_kv // 2):
            k2 = _dup_halves(mkv_ref[:, pair * LANES:(pair + 1) * LANES])
            v2 = _dup_halves(mkv_ref[:, d_kv + pair * LANES:d_kv + (pair + 1) * LANES])
            for par in range(2):
                kvh = 2 * pair + par
                kbuf[kvh, meta0:meta0 + n_meta, :] = k2[par].astype(BF16)
                vbuf[kvh, meta0:meta0 + n_meta, :] = v2[par].astype(BF16)
                kbuf[kvh, w:2 * w, :] = jnp.zeros((w, LANES), BF16)
                vbuf[kvh, w:2 * w, :] = jnp.zeros((w, LANES), BF16)

    t = lax.broadcasted_iota(jnp.int32, (4 * w, w), 0) & (w - 1)
    causal = lax.broadcasted_iota(jnp.int32, (4 * w, w), 1) <= t
    first_pen = jnp.where(b == 0, NEG_INF, 0.0)

    for sub in range(2):
        rows = slice(sub * w, (sub + 1) * w)
        cur, prev = sub, 1 - sub
        for pair in range(n_kv // 2):
            k2 = _dup_halves(kv_ref[rows, pair * LANES:(pair + 1) * LANES])
            v2 = _dup_halves(kv_ref[rows, d_kv + pair * LANES:d_kv + (pair + 1) * LANES])
            for par in range(2):
                kbuf[2 * pair + par, cur * w:(cur + 1) * w, :] = k2[par].astype(BF16)
                vbuf[2 * pair + par, cur * w:(cur + 1) * w, :] = v2[par].astype(BF16)

        for kvh in range(n_kv):
            q0, q1 = _q_tiles(q_ref, rows, kvh)
            s = lax.dot_general(_stack_heads(q0, q1), kbuf[kvh], NT_DIMS, preferred_element_type=F32)
            s_cur = s[:, cur * w:(cur + 1) * w]
            s_prev = s[:, prev * w:(prev + 1) * w]
            s_meta = s[:, meta0:meta0 + n_meta]
            if sub == 0:
                s_prev = s_prev + first_pen
            band = jnp.where(causal, s_cur, s_prev)
            sink_col = _sink_col(sinks_ref, 4 * kvh, w)
            m = jnp.maximum(jnp.maximum(jnp.max(band, axis=1, keepdims=True),
                                        jnp.max(s_meta, axis=1, keepdims=True)), sink_col)
            p_band = jnp.exp2(band - m)
            p_meta = jnp.exp2(s_meta - m)
            denom = (jnp.sum(p_band, axis=1, keepdims=True) + jnp.sum(p_meta, axis=1, keepdims=True)
                     + jnp.exp2(sink_col - m))
            p_cur = jnp.where(causal, p_band, 0.0).astype(BF16)
            p_prev = jnp.where(causal, 0.0, p_band).astype(BF16)
            slots = (p_cur, p_prev) if cur == 0 else (p_prev, p_cur)
            p = jnp.concatenate(slots + (p_meta.astype(BF16),), axis=1)
            pv = jnp.dot(p, vbuf[kvh], preferred_element_type=F32) / denom
            o0, o1 = _unstack_heads(pv, w)
            _gate_store(o_ref, sga_ref, rows, kvh, o0, o1)


def _attn_sample_kernel(sinks_ref, q_ref, kv_ref, sga_ref, mk_ref, mv_ref, wk_ref, wv_ref,
                        o_ref, wko_ref, wvo_ref, *, n_kv):
    n_seq, win, d_kv = wk_ref.shape
    n_meta = mk_ref.shape[1]
    t_new = q_ref.shape[0] // n_seq
    pad = (-(win + n_meta + t_new)) % 16
    nk = win + n_meta + t_new + pad

    j = lax.broadcasted_iota(jnp.int32, (4 * t_new, nk), 0) & (t_new - 1)
    c = lax.broadcasted_iota(jnp.int32, (4 * t_new, nk), 1)
    new0 = win + n_meta
    mask = ((c < win) & (c > j)) | ((c >= win) & (c < new0)) | ((c >= new0) & (c - new0 <= j))
    zpad = jnp.zeros((pad, LANES), F32)

    def seq_body(s, carry):
        rows = pl.ds(pl.multiple_of(s * t_new, t_new), t_new)
        knew = kv_ref[rows, 0:d_kv]
        vnew = kv_ref[rows, d_kv:2 * d_kv]
        wk = wk_ref[s]
        wv = wv_ref[s]
        wko_ref[s, 0:win - t_new, :] = wk[t_new:win]
        wko_ref[s, win - t_new:win, :] = knew
        wvo_ref[s, 0:win - t_new, :] = wv[t_new:win]
        wvo_ref[s, win - t_new:win, :] = vnew
        mk = mk_ref[s]
        mv = mv_ref[s]
        for pair in range(n_kv // 2):
            cols = slice(pair * LANES, (pair + 1) * LANES)
            pieces_k = [wk[:, cols], mk[:, cols], knew[:, cols]]
            pieces_v = [wv[:, cols], mv[:, cols], vnew[:, cols]]
            if pad:
                pieces_k.append(zpad)
                pieces_v.append(zpad)
            k2 = _dup_halves(jnp.concatenate(pieces_k, axis=0))
            v2 = _dup_halves(jnp.concatenate(pieces_v, axis=0))
            for par in range(2):
                kvh = 2 * pair + par
                q0, q1 = _q_tiles(q_ref, rows, kvh)
                o0, o1 = _attn_core(q0, q1, k2[par].astype(BF16), v2[par].astype(BF16), mask,
                                    _sink_col(sinks_ref, 4 * kvh, t_new))
                _gate_store(o_ref, sga_ref, rows, kvh, o0, o1)
        return carry

    lax.fori_loop(0, n_seq, seq_body, 0, unroll=2)


def _smem_spec():
    return pl.BlockSpec(memory_space=pltpu.SMEM)


def _attn_meta(sinks, q, kv, sga, *, n_kv):
    n, d = q.shape
    return pl.pallas_call(
        functools.partial(_attn_meta_kernel, n_kv=n_kv),
        grid=(1,),
        in_specs=[_smem_spec(), _small(q), _small(kv), _small(sga)],
        out_specs=_small(q),
        out_shape=jax.ShapeDtypeStruct((n, d), q.dtype),
        compiler_params=_cparams(),
        name="attn_meta",
    )(sinks, q, kv, sga)


def _attn_prompt(sinks, q, kv, sga, mkv, *, n_kv, window):
    t, d = q.shape
    n_meta = mkv.shape[0]
    nk = n_meta + 2 * window
    tb = 2 * window
    assert t % tb == 0
    return pl.pallas_call(
        functools.partial(_attn_prompt_kernel, n_kv=n_kv, n_meta=n_meta, window=window),
        grid=(t // tb,),
        in_specs=[_smem_spec(), _rows(d, tb), _rows(kv.shape[1], tb), _rows(d, tb), _small(mkv)],
        out_specs=_rows(d, tb),
        out_shape=jax.ShapeDtypeStruct((t, d), q.dtype),
        scratch_shapes=[pltpu.VMEM((n_kv, nk, LANES), BF16), pltpu.VMEM((n_kv, nk, LANES), BF16)],
        compiler_params=_cparams(),
        name="attn_prompt",
    )(sinks, q, kv, sga, mkv)


def _attn_sample(sinks, q, kv, sga, mk, mv, wk, wv, *, n_kv, seq_block):
    t, d = q.shape
    n_seq, win, d_kv = wk.shape
    t_new = t // n_seq
    tb = seq_block * t_new
    cache = lambda a: pl.BlockSpec((seq_block,) + a.shape[1:], lambda i: (i, 0, 0))
    return pl.pallas_call(
        functools.partial(_attn_sample_kernel, n_kv=n_kv),
        grid=(n_seq // seq_block,),
        in_specs=[_smem_spec(), _rows(d, tb), _rows(kv.shape[1], tb), _rows(d, tb),
                  cache(mk), cache(mv), cache(wk), cache(wv)],
        out_specs=[_rows(d, tb), cache(wk), cache(wv)],
        out_shape=(jax.ShapeDtypeStruct((t, d), q.dtype), jax.ShapeDtypeStruct(wk.shape, F32),
                   jax.ShapeDtypeStruct(wv.shape, F32)),
        compiler_params=_cparams(),
        name="attn_sample",
    )(sinks, q, kv, sga, mk, mv, wk, wv)


def _causal(chunk):
    r = lax.broadcasted_iota(jnp.int32, (chunk, chunk), 0)
    c = lax.broadcasted_iota(jnp.int32, (chunk, chunk), 1)
    return c <= r


def _merge_store(m_ref, o, ng_ref, gg_ref, a_ref):
    on = o * lax.rsqrt(jnp.mean(o * o, axis=-1, keepdims=True) + RMS_EPS) * ng_ref[...]
    merged = a_ref[...].astype(F32) + on * gg_ref[...].astype(F32)
    m_ref[...] = merged.astype(m_ref.dtype)


def _hgrn_seq_kernel(ng_ref, qd_ref, kd_ref, dec_ref, hi_ref, gg_ref, a_ref, sinit_ref,
                     m_ref, sout_ref, st_scr, o_scr, *, chunk, n_heads):
    i = pl.program_id(0)
    rb, d_h = hi_ref.shape
    dk = d_h // n_heads

    @pl.when(i == 0)
    def _():
        st_scr[...] = sinit_ref[...]

    causal = _causal(chunk)
    for c in range(rb // chunk):
        rows = slice(c * chunk, (c + 1) * chunk)
        decay = dec_ref[c:c + 1, :]
        qd = qd_ref[rows, :].astype(BF16)
        kd = kd_ref[rows, :]
        ke = (kd.astype(F32) * decay).astype(BF16)
        kd = kd.astype(BF16)
        v = hi_ref[rows, :].astype(BF16)
        for h in range(n_heads):
            sl = slice(h * dk, (h + 1) * dk)
            a = lax.dot_general(qd[:, sl], kd[:, sl], NT_DIMS, preferred_element_type=F32)
            a = jnp.where(causal, a, 0.0).astype(BF16)
            st = st_scr[h]
            o = jnp.dot(a, v[:, sl], preferred_element_type=F32)
            o = o + lax.dot_general(qd[:, sl], st.astype(BF16), NT_DIMS, preferred_element_type=F32)
            o_scr[rows, sl] = o
            st_scr[h] = st * decay[:, sl] + lax.dot_general(v[:, sl], ke[:, sl], TN_DIMS,
                                                            preferred_element_type=F32)

    _merge_store(m_ref, o_scr[...], ng_ref, gg_ref, a_ref)
    sout_ref[...] = st_scr[...]


def _hgrn_sample_kernel(ng_ref, qd_ref, kd_ref, dec_ref, hi_ref, gg_ref, a_ref, s0_ref,
                        m_ref, sout_ref, o_scr):
    n_seq, n_heads, dk, dv = s0_ref.shape
    rb, d_h = hi_ref.shape
    t_new = rb // n_seq
    causal = _causal(t_new)

    def seq_body(si, carry):
        rows = pl.ds(pl.multiple_of(si * t_new, t_new), t_new)
        decay = dec_ref[pl.ds(si, 1), :]
        qd = qd_ref[rows, :].astype(BF16)
        kd = kd_ref[rows, :]
        ke = (kd * decay).astype(BF16)
        kd = kd.astype(BF16)
        v = hi_ref[rows, :].astype(BF16)
        for h in range(n_heads):
            sl = slice(h * dk, (h + 1) * dk)
            a = lax.dot_general(qd[:, sl], kd[:, sl], NT_DIMS, preferred_element_type=F32)
            a = jnp.where(causal, a, 0.0).astype(BF16)
            s0 = s0_ref[si, h]
            o = jnp.dot(a, v[:, sl], preferred_element_type=F32)
            o = o + jnp.dot(qd[:, sl], s0.astype(BF16), preferred_element_type=F32)
            o_scr[rows, sl] = o
            decay_t = jnp.broadcast_to(decay[:, sl], (dk, dk)).T
            sout_ref[si, h] = s0 * decay_t + lax.dot_general(ke[:, sl], v[:, sl], TN_DIMS,
                                                             preferred_element_type=F32)
        return carry

    lax.fori_loop(0, n_seq, seq_body, 0, unroll=2)
    _merge_store(m_ref, o_scr[...], ng_ref, gg_ref, a_ref)


def _hgrn_seq(ng, qd, kd, dec, hi, gg, a, sinit, *, rb, chunk):
    t, d_h = hi.shape
    n_heads = sinit.shape[0]
    return pl.pallas_call(
        functools.partial(_hgrn_seq_kernel, chunk=chunk, n_heads=n_heads),
        grid=(t // rb,),
        in_specs=[_small(ng), _rows(d_h, rb), _rows(d_h, rb), _rows(d_h, rb // chunk), _rows(d_h, rb),
                  _rows(d_h, rb), _rows(d_h, rb), _small(sinit)],
        out_specs=[_rows(d_h, rb), _small(sinit)],
        out_shape=(jax.ShapeDtypeStruct((t, d_h), a.dtype), jax.ShapeDtypeStruct(sinit.shape, F32)),
        scratch_shapes=[pltpu.VMEM(sinit.shape, F32), pltpu.VMEM((rb, d_h), F32)],
        compiler_params=_cparams(),
        name="hgrn_seq",
    )(ng, qd, kd, dec, hi, gg, a, sinit)


def _hgrn_sample(ng, qd, kd, dec, hi, gg, a, s0, *, seq_block):
    t, d_h = hi.shape
    n_seq = s0.shape[0]
    rb = seq_block * (t // n_seq)
    state = pl.BlockSpec((seq_block,) + s0.shape[1:], lambda i: (i, 0, 0, 0))
    return pl.pallas_call(
        _hgrn_sample_kernel,
        grid=(n_seq // seq_block,),
        in_specs=[_small(ng), _rows(d_h, rb), _rows(d_h, rb), _rows(d_h, seq_block), _rows(d_h, rb),
                  _rows(d_h, rb), _rows(d_h, rb), state],
        out_specs=[_rows(d_h, rb), state],
        out_shape=(jax.ShapeDtypeStruct((t, d_h), a.dtype), jax.ShapeDtypeStruct(s0.shape, F32)),
        scratch_shapes=[pltpu.VMEM((rb, d_h), F32)],
        compiler_params=_cparams(),
        name="hgrn_sample",
    )(ng, qd, kd, dec, hi, gg, a, s0)


def _layer_norm(v, g_ref, b_ref):
    mu = jnp.mean(v, axis=-1, keepdims=True)
    c = v - mu
    var = jnp.mean(c * c, axis=-1, keepdims=True)
    return c * lax.rsqrt(var + LN_EPS) * g_ref[...] + b_ref[...]


def _outffn_kernel(x_ref, m_ref, hist_ref, wo_ref, g1_ref, b1_ref, wu_ref, cw_ref, cb_ref, wd_ref,
                   g2_ref, b2_ref, y_ref, tail_ref, ubuf, h_scr, *, alpha, shift):
    i = pl.program_id(0)
    tb = x_ref.shape[0]
    d_ff = wd_ref.shape[0]
    nh = hist_ref.shape[0]
    hpad = ubuf.shape[0] - tb

    @pl.when(i == 0)
    def _():
        ubuf[hpad - nh:hpad, :] = hist_ref[...]

    x = x_ref[...]
    mix = jnp.dot(m_ref[...].astype(BF16), wo_ref[...], preferred_element_type=F32)
    y1 = _layer_norm(alpha * x + mix, g1_ref, b1_ref)
    y1b = y1.astype(BF16)

    for c0 in range(0, d_ff, FF_COLS):
        cols = slice(c0, c0 + FF_COLS)
        u = jnp.dot(y1b, wu_ref[:, cols], preferred_element_type=F32)
        val = jnp.dot(y1b, wu_ref[:, d_ff + c0:d_ff + c0 + FF_COLS], preferred_element_type=F32)
        ubuf[hpad:hpad + tb, cols] = u
        u1 = ubuf[hpad - shift:hpad - shift + tb, cols]
        u2 = ubuf[hpad - 2 * shift:hpad - 2 * shift + tb, cols]
        conv = cb_ref[:, cols] + cw_ref[0:1, cols] * u2 + cw_ref[1:2, cols] * u1 + cw_ref[2:3, cols] * u
        h_scr[:, cols] = (jax.nn.gelu(conv) * val).astype(BF16)

    ffn = jnp.dot(h_scr[...], wd_ref[...], preferred_element_type=F32)
    y_ref[...] = _layer_norm(alpha * y1 + ffn, g2_ref, b2_ref)
    tail = ubuf[hpad + tb - nh:hpad + tb, :]
    tail_ref[...] = tail
    ubuf[hpad - nh:hpad, :] = tail


def _outffn(x, m, hist, wo, g1, b1, wu, cw, cb, wd, g2, b2, *, tb, alpha, shift):
    t, d = x.shape
    d_ff = wd.shape[0]
    assert t % tb == 0 and d_ff % FF_COLS == 0 and hist.shape == (2 * shift, d_ff) and tb >= 2 * shift
    hpad = -(-2 * shift // 8) * 8
    return pl.pallas_call(
        functools.partial(_outffn_kernel, alpha=alpha, shift=shift),
        grid=(t // tb,),
        in_specs=[_rows(d, tb), _rows(d, tb), _small(hist), _resident(wo.shape), _small(g1), _small(b1),
                  _resident(wu.shape), _small(cw), _small(cb), _resident(wd.shape), _small(g2), _small(b2)],
        out_specs=[_rows(d, tb), _small(hist)],
        out_shape=(jax.ShapeDtypeStruct((t, d), F32), jax.ShapeDtypeStruct(hist.shape, F32)),
        scratch_shapes=[pltpu.VMEM((hpad + tb, d_ff), F32), pltpu.VMEM((tb, d_ff), BF16)],
        compiler_params=_cparams(),
        name="outffn",
    )(x, m, hist, wo, g1, b1, wu, cw, cb, wd, g2, b2)


def kernel(x_prompt, x_sample, cache_meta_k, cache_meta_v, cache_win_k, cache_win_v, state_hgrn, state_conv,
           meta_tokens, w_in, attn_sinks, hgrn_lb, hgrn_norm_g, w_out, ln1_g, ln1_b,
           w_up, conv_w, conv_b, w_down, ln2_g, ln2_b):
    depth = w_in.shape[0]
    assert depth == 1, "only a single layer is supported"
    layer = 0
    batch, seq, d = x_prompt.shape
    assert batch == 1
    n_seq, t_new, _ = x_sample.shape
    _, _, n_meta, n_kv, head_dim = cache_meta_k.shape
    window = cache_win_k.shape[2]
    n_heads = attn_sinks.shape[1]
    _, _, hg_heads, hg_k, hg_v = state_hgrn.shape
    d_ff = w_down.shape[1]
    d_attn = n_heads * head_dim
    d_kv = n_kv * head_dim
    d_h = hg_heads * hg_k
    assert head_dim == HALF and n_heads == 4 * n_kv and n_kv % 2 == 0 and hg_k == hg_v == LANES
    assert d_attn == d and d_h == d and window == LANES and seq % ROW_BLOCK == 0
    alpha = (2.0 * depth) ** 0.25

    wi = w_in[layer].astype(BF16)
    wo = w_out[layer].astype(BF16)
    wu = w_up[layer].astype(BF16)
    wd = w_down[layer].astype(BF16)
    sinks = attn_sinks[layer].astype(F32)
    ng = hgrn_norm_g[layer].reshape(1, d_h)
    g1, b1 = ln1_g[layer].reshape(1, d), ln1_b[layer].reshape(1, d)
    g2, b2 = ln2_g[layer].reshape(1, d), ln2_b[layer].reshape(1, d)
    cw, cb = conv_w[layer], conv_b[layer].reshape(1, d_ff)

    inproj = functools.partial(_inproj, hgrn_lb, w=wi, d_attn=d_attn, d_kv2=2 * d_kv, d_h=d_h,
                               q_scale=head_dim ** -0.5 * LOG2E, layer=layer)
    outffn = functools.partial(_outffn, wo=wo, g1=g1, b1=b1, wu=wu, cw=cw, cb=cb, wd=wd, g2=g2, b2=b2, alpha=alpha)

    xm = meta_tokens.astype(F32)
    q_m, kv_m, qd_m, kd_m, dec_m, hi_m, sga_m, gg_m = inproj(xm, tb=n_meta, chunk=n_meta, act_dtype=BF16)
    a_m = _attn_meta(sinks, q_m, kv_m, sga_m, n_kv=n_kv)
    merged_m, st_m = _hgrn_seq(ng, qd_m, kd_m, dec_m, hi_m, gg_m, a_m, jnp.zeros((hg_heads, hg_v, hg_k), F32),
                               rb=n_meta, chunk=n_meta)
    _, tail_m = outffn(xm, merged_m, jnp.zeros((2, d_ff), F32), tb=n_meta, shift=1)

    xp = x_prompt[0]
    q_p, kv_p, qd_p, kd_p, dec_p, hi_p, sga_p, gg_p = inproj(xp, tb=ROW_BLOCK, chunk=HG_CHUNK, act_dtype=BF16)
    a_p = _attn_prompt(sinks, q_p, kv_p, sga_p, kv_m, n_kv=n_kv, window=window)
    merged_p, st_p = _hgrn_seq(ng, qd_p, kd_p, dec_p, hi_p, gg_p, a_p, st_m, rb=ROW_BLOCK, chunk=HG_CHUNK)
    y_p, tail_p = outffn(xp, merged_p, tail_m, tb=ROW_BLOCK, shift=1)

    xs = x_sample.reshape(n_seq * t_new, d)
    tb_s = min(ROW_BLOCK, n_seq * t_new)
    q_s, kv_s, qd_s, kd_s, dec_s, hi_s, sga_s, gg_s = inproj(xs, tb=tb_s, chunk=t_new, act_dtype=F32)
    a_s, wk_s, wv_s = _attn_sample(
        sinks, q_s, kv_s, sga_s,
        cache_meta_k[layer].reshape(n_seq, n_meta, d_kv), cache_meta_v[layer].reshape(n_seq, n_meta, d_kv),
        cache_win_k[layer].reshape(n_seq, window, d_kv), cache_win_v[layer].reshape(n_seq, window, d_kv),
        n_kv=n_kv, seq_block=8)
    merged_s, st_s = _hgrn_sample(ng, qd_s, kd_s, dec_s, hi_s, gg_s, a_s, state_hgrn[layer], seq_block=8)
    time_major = lambda a: a.reshape(n_seq, t_new, d).transpose(1, 0, 2).reshape(t_new * n_seq, d)
    hist_s = state_conv[layer].transpose(1, 0, 2).reshape(2 * n_seq, d_ff)
    y_s, tail_s = outffn(time_major(xs), time_major(merged_s), hist_s, tb=tb_s, shift=n_seq)

    kv_heads = lambda a, rows: a.reshape(1, -1, rows, n_kv, head_dim)
    return (
        y_p[None],
        y_s.reshape(t_new, n_seq, d).transpose(1, 0, 2),
        kv_heads(kv_m[:, :d_kv], n_meta),
        kv_heads(kv_m[:, d_kv:], n_meta),
        kv_heads(kv_p[seq - window:, :d_kv], window),
        kv_heads(kv_p[seq - window:, d_kv:], window),
        st_p.transpose(0, 2, 1)[None, None],
        tail_p[None, None],
        kv_heads(wk_s, window),
        kv_heads(wv_s, window),
        st_s[None],
        tail_s.reshape(2, n_seq, d_ff).transpose(1, 0, 2)[None],
    )
```

```python
import functools
import math

import jax
import jax.numpy as jnp
import numpy as np
from jax import lax
from jax.experimental import pallas as pl
from jax.experimental.pallas import tpu as pltpu

F32 = jnp.float32
BF16 = jnp.bfloat16

HG_CHUNK = 64
LN_EPS = 1e-5
RMS_EPS = 1e-6
NEG_INF = -1e30
LOG2E = math.log2(math.e)

LANES = 128
HALF = LANES // 2
VMEM_LIMIT_BYTES = 56 * 1024 * 1024

DOT_COLS = 512
FF_COLS = 256
ROW_BLOCK = 512
SEQ_UNROLL = 4
GATE_SLAB = 128

NT_DIMS = (((1,), (1,)), ((), ()))
TN_DIMS = (((0,), (0,)), ((), ()))


def _cparams():
    return pltpu.CompilerParams(dimension_semantics=("arbitrary",), vmem_limit_bytes=VMEM_LIMIT_BYTES)


def _resident(shape):
    nd = len(shape)
    return pl.BlockSpec(shape, lambda i: (0,) * nd, pipeline_mode=pl.Buffered(1))


def _small(a):
    nd = a.ndim
    return pl.BlockSpec(a.shape, lambda i: (0,) * nd)


def _rows(cols, tb):
    return pl.BlockSpec((tb, cols), lambda i: (i, 0))


def _sigmoid(x):
    return 0.5 * jnp.tanh(0.5 * x) + 0.5


def _layer_lower_bound(lb_ref, layer):
    r = lb_ref[...]
    e = jnp.exp(r - jnp.max(r, axis=0, keepdims=True))
    return jnp.sum(e[0:layer + 1], axis=0, keepdims=True) / jnp.sum(e, axis=0, keepdims=True)


def _chunk_cumprod(x, chunk):
    r = lax.broadcasted_iota(jnp.int32, x.shape, 0) & (chunk - 1)
    s = 1
    while s < chunk:
        x = x * jnp.where(r >= s, pltpu.roll(x, s, axis=0), 1.0)
        s *= 2
    return x


def _inproj_kernel(lb_ref, x_ref, w_ref, q_ref, kv_ref, qd_ref, kd_ref, dec_ref, hi_ref, sga_ref, gg_ref, eb_scr,
                   *, d_attn, d_kv2, d_h, q_scale, chunk, layer):
    xb = x_ref[...].astype(BF16)
    tb = x_ref.shape[0]

    def mm(c0, n):
        return jnp.dot(xb, w_ref[:, c0:c0 + n], preferred_element_type=F32)

    off_kv = d_attn
    off_hq = off_kv + d_kv2
    off_hf, off_hi, off_hg, off_ga, off_gb = (off_hq + j * d_h for j in range(1, 6))

    lb = _layer_lower_bound(lb_ref, layer)
    slab = min(GATE_SLAB, tb)

    def gate_slab(c, hq, hf, r0):
        cs = slice(c, c + DOT_COLS)
        rows = slice(r0, r0 + slab)
        hq, hf = hq[rows], hf[rows]
        f = lb[:, cs] + (1.0 - lb[:, cs]) * _sigmoid(hf)
        eb = _chunk_cumprod(f, chunk)
        qd_ref[rows, cs] = (hq * _sigmoid(hq) * eb).astype(qd_ref.dtype)
        kd_ref[rows, cs] = ((1.0 - f) / eb).astype(kd_ref.dtype)
        for j in range(DOT_COLS // LANES):
            tile = (c // LANES) + j
            eb_scr[tile, rows, :] = eb[:, j * LANES:(j + 1) * LANES]
            dec_ref[r0 // chunk:(r0 + slab) // chunk, tile * LANES:(tile + 1) * LANES] = (
                eb_scr[tile, pl.ds(r0 + chunk - 1, slab // chunk, stride=chunk), :])

    def q_dot(c):
        q_ref[:, c:c + DOT_COLS] = (mm(c, DOT_COLS) * q_scale).astype(q_ref.dtype)

    def kv_dot():
        kv_ref[...] = mm(off_kv, d_kv2)

    def hi_dot(c):
        hi_ref[:, c:c + DOT_COLS] = mm(off_hi + c, DOT_COLS).astype(hi_ref.dtype)

    def ga_dot(c):
        sga_ref[:, c:c + DOT_COLS] = _sigmoid(mm(off_ga + c, DOT_COLS)).astype(sga_ref.dtype)

    def gg_dots(c):
        hg = mm(off_hg + c, DOT_COLS)
        gg_ref[:, c:c + DOT_COLS] = (_sigmoid(mm(off_gb + c, DOT_COLS)) * hg * _sigmoid(hg)).astype(gg_ref.dtype)

    fillers = [functools.partial(q_dot, c) for c in range(0, d_attn, DOT_COLS)] + [kv_dot]
    for c in range(0, d_h, DOT_COLS):
        fillers += [functools.partial(hi_dot, c), functools.partial(ga_dot, c), functools.partial(gg_dots, c)]
    for c in range(0, d_h, DOT_COLS):
        hq = mm(off_hq + c, DOT_COLS)
        hf = mm(off_hf + c, DOT_COLS)
        for r0 in range(0, tb, slab):
            gate_slab(c, hq, hf, r0)
            if fillers:
                fillers.pop(0)()
    for filler in fillers:
        filler()


def _inproj(lbp, x, w, *, tb, chunk, act_dtype, d_attn, d_kv2, d_h, q_scale, layer):
    t, d = x.shape
    assert t % tb == 0 and tb % chunk == 0 and d_attn % DOT_COLS == 0 and d_h % DOT_COLS == 0
    act = lambda cols: jax.ShapeDtypeStruct((t, cols), act_dtype)
    outs = (act(d_attn), jax.ShapeDtypeStruct((t, d_kv2), F32), act(d_h), act(d_h),
            jax.ShapeDtypeStruct((t // chunk, d_h), F32), act(d_h), act(d_h), act(d_h))
    out_specs = [_rows(o.shape[1], tb) for o in outs]
    out_specs[4] = _rows(d_h, tb // chunk)
    return pl.pallas_call(
        functools.partial(_inproj_kernel, d_attn=d_attn, d_kv2=d_kv2, d_h=d_h, q_scale=q_scale, chunk=chunk,
                          layer=layer),
        grid=(t // tb,),
        in_specs=[_small(lbp), _rows(d, tb), _resident(w.shape)],
        out_specs=out_specs,
        out_shape=outs,
        scratch_shapes=[pltpu.VMEM((d_h // LANES, tb, LANES), F32)],
        compiler_params=_cparams(),
        name="inproj",
    )(lbp, x, w)


def _low_lanes(shape):
    return lax.broadcasted_iota(jnp.int32, shape, len(shape) - 1) < HALF


def _dup_halves(tile):
    rolled = pltpu.roll(tile, HALF, axis=1)
    lo = _low_lanes(tile.shape)
    return jnp.where(lo, tile, rolled), jnp.where(lo, rolled, tile)


def _sink_col(sinks_ref, head0, n):
    r = lax.broadcasted_iota(jnp.int32, (4 * n, 1), 0)
    col = jnp.full((4 * n, 1), sinks_ref[head0 + 3] * LOG2E, F32)
    for j in (2, 1, 0):
        col = jnp.where(r < (j + 1) * n, sinks_ref[head0 + j] * LOG2E, col)
    return col


def _stack_heads(q0, q1):
    lo = _low_lanes(q0.shape)
    zero = jnp.zeros(q0.shape, F32)
    return jnp.concatenate(
        [jnp.where(lo, q0, zero), jnp.where(lo, zero, q0), jnp.where(lo, q1, zero), jnp.where(lo, zero, q1)],
        axis=0).astype(BF16)


def _unstack_heads(pv, n):
    lo = _low_lanes((n, LANES))
    return jnp.where(lo, pv[0:n], pv[n:2 * n]), jnp.where(lo, pv[2 * n:3 * n], pv[3 * n:4 * n])


def _attn_core(q0, q1, kb, vb, mask, sink_col):
    n = q0.shape[0]
    s = lax.dot_general(_stack_heads(q0, q1), kb, NT_DIMS, preferred_element_type=F32)
    s = jnp.where(mask, s, NEG_INF)
    m = jnp.maximum(jnp.max(s, axis=1, keepdims=True), sink_col)
    p = jnp.exp2(s - m)
    return _unstack_heads(_pv_normalised(p.astype(BF16), _with_ones(vb), sink_col, m), n)


def _with_ones(vb):
    return jnp.concatenate([vb, jnp.ones(vb.shape, vb.dtype)], axis=1)


def _pv_normalised(p, vb_ones, sink_col, m):
    pv = jnp.dot(p, vb_ones, preferred_element_type=F32)
    return pv[:, 0:LANES] / (pv[:, LANES:2 * LANES] + jnp.exp2(sink_col - m))


def _gate_store(o_ref, sga_ref, rows, kvh, o0, o1):
    for j, o in enumerate((o0, o1)):
        cols = slice((2 * kvh + j) * LANES, (2 * kvh + j + 1) * LANES)
        o_ref[rows, cols] = (sga_ref[rows, cols].astype(F32) * o).astype(o_ref.dtype)


def _q_tiles(q_ref, rows, kvh):
    return (q_ref[rows, (2 * kvh) * LANES:(2 * kvh + 1) * LANES].astype(F32),
            q_ref[rows, (2 * kvh + 1) * LANES:(2 * kvh + 2) * LANES].astype(F32))


def _attn_meta_kernel(sinks_ref, q_ref, kv_ref, sga_ref, o_ref, *, n_kv):
    n = q_ref.shape[0]
    d_kv = n_kv * HALF
    r = lax.broadcasted_iota(jnp.int32, (4 * n, n), 0) & (n - 1)
    c = lax.broadcasted_iota(jnp.int32, (4 * n, n), 1)
    mask = c <= r
    rows = slice(0, n)
    for pair in range(n_kv // 2):
        k2 = _dup_halves(kv_ref[:, pair * LANES:(pair + 1) * LANES])
        v2 = _dup_halves(kv_ref[:, d_kv + pair * LANES:d_kv + (pair + 1) * LANES])
        for par in range(2):
            kvh = 2 * pair + par
            q0, q1 = _q_tiles(q_ref, rows, kvh)
            o0, o1 = _attn_core(q0, q1, k2[par].astype(BF16), v2[par].astype(BF16), mask,
                                _sink_col(sinks_ref, 4 * kvh, n))
            _gate_store(o_ref, sga_ref, rows, kvh, o0, o1)


def _attn_prompt_init(mkv_ref, kbuf, vbuf, *, n_kv, n_meta, w):
    d_kv = n_kv * HALF
    nk = kbuf.shape[1]
    for pair in range(n_kv // 2):
        k2 = _dup_halves(mkv_ref[:, pair * LANES:(pair + 1) * LANES])
        v2 = _dup_halves(mkv_ref[:, d_kv + pair * LANES:d_kv + (pair + 1) * LANES])
        for par in range(2):
            kvh = 2 * pair + par
            kbuf[kvh, w:nk, :] = jnp.zeros((nk - w, LANES), BF16)
            vbuf[kvh, w:nk, 0:LANES] = jnp.zeros((nk - w, LANES), BF16)
            vbuf[kvh, :, LANES:2 * LANES] = jnp.ones((nk, LANES), BF16)
            kbuf[kvh, 2 * w:2 * w + n_meta, :] = k2[par].astype(BF16)
            vbuf[kvh, 2 * w:2 * w + n_meta, 0:LANES] = v2[par].astype(BF16)


def _attn_prompt_block(sub, first_pen, causal, sinks_ref, q_ref, kv_ref, sga_ref, o_ref, kbuf, vbuf,
                       *, n_kv, n_meta, w):
    d_kv = n_kv * HALF
    rows = slice(sub * w, (sub + 1) * w)
    cur, prev = sub % 2, 1 - sub % 2
    for pair in range(n_kv // 2):
        k2 = _dup_halves(kv_ref[rows, pair * LANES:(pair + 1) * LANES])
        v2 = _dup_halves(kv_ref[rows, d_kv + pair * LANES:d_kv + (pair + 1) * LANES])
        for par in range(2):
            kbuf[2 * pair + par, cur * w:(cur + 1) * w, :] = k2[par].astype(BF16)
            vbuf[2 * pair + par, cur * w:(cur + 1) * w, 0:LANES] = v2[par].astype(BF16)

    is_meta = lax.broadcasted_iota(jnp.int32, (4 * w, LANES), 1) < n_meta
    for kvh in range(n_kv):
        q0, q1 = _q_tiles(q_ref, rows, kvh)
        s = lax.dot_general(_stack_heads(q0, q1), kbuf[kvh], NT_DIMS, preferred_element_type=F32)
        s_cur = s[:, cur * w:(cur + 1) * w]
        s_prev = s[:, prev * w:(prev + 1) * w]
        s_meta = jnp.where(is_meta, s[:, 2 * w:2 * w + LANES], NEG_INF)
        if first_pen is not None:
            s_prev = s_prev + first_pen
        band = jnp.where(causal, s_cur, s_prev)
        sink_col = _sink_col(sinks_ref, 4 * kvh, w)
        m = jnp.maximum(jnp.max(jnp.maximum(band, s_meta), axis=1, keepdims=True), sink_col)
        p_band = jnp.exp2(band - m)
        p_cur = jnp.where(causal, p_band, 0.0).astype(BF16)
        p_prev = jnp.where(causal, 0.0, p_band).astype(BF16)
        slots = (p_cur, p_prev) if cur == 0 else (p_prev, p_cur)
        p = jnp.concatenate(slots + (jnp.exp2(s_meta - m).astype(BF16),), axis=1)
        o0, o1 = _unstack_heads(_pv_normalised(p, vbuf[kvh], sink_col, m), w)
        _gate_store(o_ref, sga_ref, rows, kvh, o0, o1)


def _attn_sample_kernel(sinks_ref, bias_ref, q_ref, kv_ref, sga_ref, mk_ref, mv_ref, wk_ref, wv_ref,
                        o_ref, wko_ref, wvo_ref, kbuf, vbuf, *, n_kv):
    n_seq, win, d_kv = wk_ref.shape
    n_meta = mk_ref.shape[1]
    rb = q_ref.shape[0]
    t_new = rb // n_seq
    nk = kbuf.shape[1] // n_seq
    pad = nk - (win + n_meta + t_new)
    zpad = jnp.zeros((pad, LANES), F32)

    @pl.when(pl.program_id(0) == 0)
    def _():
        vbuf[:, :, LANES:2 * LANES] = jnp.ones(vbuf.shape[:2] + (LANES,), BF16)

    def seq_body(s, carry):
        rows = pl.ds(pl.multiple_of(s * t_new, t_new), t_new)
        keys = pl.ds(pl.multiple_of(s * nk, nk), nk)
        knew = kv_ref[rows, 0:d_kv]
        vnew = kv_ref[rows, d_kv:2 * d_kv]
        wk = wk_ref[s]
        wv = wv_ref[s]
        wko_ref[s, 0:win - t_new, :] = wk[t_new:win]
        wko_ref[s, win - t_new:win, :] = knew
        wvo_ref[s, 0:win - t_new, :] = wv[t_new:win]
        wvo_ref[s, win - t_new:win, :] = vnew
        mk = mk_ref[s]
        mv = mv_ref[s]
        for pair in range(n_kv // 2):
            cols = slice(pair * LANES, (pair + 1) * LANES)
            pieces_k = [wk[:, cols], mk[:, cols], knew[:, cols]]
            pieces_v = [wv[:, cols], mv[:, cols], vnew[:, cols]]
            if pad:
                pieces_k.append(zpad)
                pieces_v.append(zpad)
            k2 = _dup_halves(jnp.concatenate(pieces_k, axis=0))
            v2 = _dup_halves(jnp.concatenate(pieces_v, axis=0))
            for par in range(2):
                kbuf[2 * pair + par, keys, :] = k2[par].astype(BF16)
                vbuf[2 * pair + par, keys, 0:LANES] = v2[par].astype(BF16)
        return carry

    lax.fori_loop(0, n_seq, seq_body, 0, unroll=SEQ_UNROLL)

    rows = slice(0, rb)
    for kvh in range(n_kv):
        q0, q1 = _q_tiles(q_ref, rows, kvh)
        s = lax.dot_general(_stack_heads(q0, q1), kbuf[kvh], NT_DIMS, preferred_element_type=F32) + bias_ref[...]
        sink_col = _sink_col(sinks_ref, 4 * kvh, rb)
        m = jnp.maximum(jnp.max(s, axis=1, keepdims=True), sink_col)
        p = jnp.exp2(s - m).astype(BF16)
        o0, o1 = _unstack_heads(_pv_normalised(p, vbuf[kvh], sink_col, m), rb)
        _gate_store(o_ref, sga_ref, rows, kvh, o0, o1)


def _sample_bias(n_seq, t_new, win, n_meta, nk):
    r = np.arange(4 * n_seq * t_new)[:, None]
    c = np.arange(n_seq * nk)[None, :]
    seq_q, tok = (r // t_new) % n_seq, r % t_new
    seq_k, idx = c // nk, c % nk
    new0 = win + n_meta
    visible = ((idx < win) & (idx > tok)) | ((idx >= win) & (idx < new0)) | ((idx >= new0) & (idx < new0 + t_new)
                                                                             & (idx - new0 <= tok))
    return np.where((seq_q == seq_k) & visible, 0.0, NEG_INF).astype(np.float32)


def _smem_spec():
    return pl.BlockSpec(memory_space=pltpu.SMEM)


def _attn_meta(sinks, q, kv, sga, *, n_kv):
    n, d = q.shape
    return pl.pallas_call(
        functools.partial(_attn_meta_kernel, n_kv=n_kv),
        grid=(1,),
        in_specs=[_smem_spec(), _small(q), _small(kv), _small(sga)],
        out_specs=_small(q),
        out_shape=jax.ShapeDtypeStruct((n, d), q.dtype),
        compiler_params=_cparams(),
        name="attn_meta",
    )(sinks, q, kv, sga)


def _mixer_prompt(sinks, ng, q, kv, sga, mkv, qd, kd, dec, hi, gg, sinit, *, rb, n_kv, window, chunk):
    t, d = q.shape
    n_meta = mkv.shape[0]
    n_heads = sinit.shape[0]
    nk = 2 * window + LANES
    assert t % rb == 0 and n_meta <= LANES
    return pl.pallas_call(
        functools.partial(_mixer_prompt_kernel, n_kv=n_kv, n_meta=n_meta, window=window, chunk=chunk,
                          n_heads=n_heads),
        grid=(t // rb,),
        in_specs=[_smem_spec(), _small(ng), _rows(d, rb), _rows(kv.shape[1], rb), _rows(d, rb), _small(mkv),
                  _rows(d, rb), _rows(d, rb), _rows(d, rb // chunk), _rows(d, rb), _rows(d, rb), _small(sinit)],
        out_specs=[_rows(d, rb), _small(sinit)],
        out_shape=(jax.ShapeDtypeStruct((t, d), q.dtype), jax.ShapeDtypeStruct(sinit.shape, F32)),
        scratch_shapes=[pltpu.VMEM((n_kv, nk, LANES), BF16), pltpu.VMEM((n_kv, nk, 2 * LANES), BF16),
                        pltpu.VMEM((rb, d), F32), pltpu.VMEM(sinit.shape, F32), pltpu.VMEM((rb, d), F32)],
        compiler_params=_cparams(),
        name="mixer_prompt",
    )(sinks, ng, q, kv, sga, mkv, qd, kd, dec, hi, gg, sinit)


def _attn_sample(sinks, q, kv, sga, mk, mv, wk, wv, *, n_kv, seq_block):
    t, d = q.shape
    n_seq, win, d_kv = wk.shape
    t_new = t // n_seq
    n_meta = mk.shape[1]
    tb = seq_block * t_new
    nk = -(-(win + n_meta + t_new) // 16) * 16
    bias = jnp.asarray(_sample_bias(seq_block, t_new, win, n_meta, nk))
    cache = lambda a: pl.BlockSpec((seq_block,) + a.shape[1:], lambda i: (i, 0, 0))
    return pl.pallas_call(
        functools.partial(_attn_sample_kernel, n_kv=n_kv),
        grid=(n_seq // seq_block,),
        in_specs=[_smem_spec(), _small(bias), _rows(d, tb), _rows(kv.shape[1], tb), _rows(d, tb),
                  cache(mk), cache(mv), cache(wk), cache(wv)],
        out_specs=[_rows(d, tb), cache(wk), cache(wv)],
        out_shape=(jax.ShapeDtypeStruct((t, d), q.dtype), jax.ShapeDtypeStruct(wk.shape, F32),
                   jax.ShapeDtypeStruct(wv.shape, F32)),
        scratch_shapes=[pltpu.VMEM((n_kv, seq_block * nk, LANES), BF16),
                        pltpu.VMEM((n_kv, seq_block * nk, 2 * LANES), BF16)],
        compiler_params=_cparams(),
        name="attn_sample",
    )(sinks, bias, q, kv, sga, mk, mv, wk, wv)


def _causal(chunk):
    r = lax.broadcasted_iota(jnp.int32, (chunk, chunk), 0)
    c = lax.broadcasted_iota(jnp.int32, (chunk, chunk), 1)
    return c <= r


def _merge_store(m_ref, o, ng_ref, gg_ref, a_ref):
    on = o * lax.rsqrt(jnp.mean(o * o, axis=-1, keepdims=True) + RMS_EPS) * ng_ref[...]
    merged = a_ref[...].astype(F32) + on * gg_ref[...].astype(F32)
    m_ref[...] = merged.astype(m_ref.dtype)


def _hgrn_chunk(c, causal, qd_ref, kd_ref, dec_ref, hi_ref, st_scr, o_scr, *, chunk, n_heads):
    dk = hi_ref.shape[1] // n_heads
    rows = slice(c * chunk, (c + 1) * chunk)
    decay = dec_ref[c:c + 1, :]
    qd = qd_ref[rows, :].astype(BF16)
    kd = kd_ref[rows, :]
    ke = (kd.astype(F32) * decay).astype(BF16)
    kd = kd.astype(BF16)
    v = hi_ref[rows, :].astype(BF16)
    for h in range(n_heads):
        sl = slice(h * dk, (h + 1) * dk)
        a = lax.dot_general(qd[:, sl], kd[:, sl], NT_DIMS, preferred_element_type=F32)
        a = jnp.where(causal, a, 0.0).astype(BF16)
        st = st_scr[h]
        o = jnp.dot(a, v[:, sl], preferred_element_type=F32)
        o = o + lax.dot_general(qd[:, sl], st.astype(BF16), NT_DIMS, preferred_element_type=F32)
        o_scr[rows, sl] = o
        st_scr[h] = st * decay[:, sl] + lax.dot_general(v[:, sl], ke[:, sl], TN_DIMS, preferred_element_type=F32)


def _hgrn_seq_kernel(ng_ref, qd_ref, kd_ref, dec_ref, hi_ref, gg_ref, a_ref, sinit_ref,
                     m_ref, sout_ref, st_scr, o_scr, *, chunk, n_heads):
    i = pl.program_id(0)
    rb, d_h = hi_ref.shape
    dk = d_h // n_heads

    @pl.when(i == 0)
    def _():
        st_scr[...] = sinit_ref[...]

    causal = _causal(chunk)
    for c in range(rb // chunk):
        _hgrn_chunk(c, causal, qd_ref, kd_ref, dec_ref, hi_ref, st_scr, o_scr, chunk=chunk, n_heads=n_heads)

    _merge_store(m_ref, o_scr[...], ng_ref, gg_ref, a_ref)
    sout_ref[...] = st_scr[...]


def _mixer_prompt_kernel(sinks_ref, ng_ref, q_ref, kv_ref, sga_ref, mkv_ref, qd_ref, kd_ref, dec_ref, hi_ref,
                         gg_ref, sinit_ref, m_ref, sout_ref, kbuf, vbuf, a_scr, st_scr, o_scr,
                         *, n_kv, n_meta, window, chunk, n_heads):
    b = pl.program_id(0)
    w = window
    rb = q_ref.shape[0]
    n_sub, n_chunk = rb // w, rb // chunk
    assert n_sub % 2 == 0 and n_chunk % n_sub == 0

    @pl.when(b == 0)
    def _():
        _attn_prompt_init(mkv_ref, kbuf, vbuf, n_kv=n_kv, n_meta=n_meta, w=w)
        st_scr[...] = sinit_ref[...]

    t = lax.broadcasted_iota(jnp.int32, (4 * w, w), 0) & (w - 1)
    attn_causal = lax.broadcasted_iota(jnp.int32, (4 * w, w), 1) <= t
    hgrn_causal = _causal(chunk)
    for sub in range(n_sub):
        first_pen = jnp.where(b == 0, NEG_INF, 0.0) if sub == 0 else None
        _attn_prompt_block(sub, first_pen, attn_causal, sinks_ref, q_ref, kv_ref, sga_ref, a_scr, kbuf, vbuf,
                           n_kv=n_kv, n_meta=n_meta, w=w)
        for c in range(sub * n_chunk // n_sub, (sub + 1) * n_chunk // n_sub):
            _hgrn_chunk(c, hgrn_causal, qd_ref, kd_ref, dec_ref, hi_ref, st_scr, o_scr, chunk=chunk, n_heads=n_heads)

    _merge_store(m_ref, o_scr[...], ng_ref, gg_ref, a_scr)
    sout_ref[...] = st_scr[...]


def _hgrn_sample_kernel(ng_ref, qd_ref, kd_ref, dec_ref, hi_ref, gg_ref, a_ref, s0_ref,
                        m_ref, sout_ref, o_scr):
    n_seq, n_heads, dk, dv = s0_ref.shape
    rb, d_h = hi_ref.shape
    t_new = rb // n_seq
    causal = _causal(t_new)

    def seq_body(si, carry):
        rows = pl.ds(pl.multiple_of(si * t_new, t_new), t_new)
        decay = dec_ref[pl.ds(si, 1), :]
        qd = qd_ref[rows, :].astype(BF16)
        kd = kd_ref[rows, :]
        ke = (kd * decay).astype(BF16)
        kd = kd.astype(BF16)
        v = hi_ref[rows, :].astype(BF16)
        for h in range(n_heads):
            sl = slice(h * dk, (h + 1) * dk)
            a = lax.dot_general(qd[:, sl], kd[:, sl], NT_DIMS, preferred_element_type=F32)
            a = jnp.where(causal, a, 0.0).astype(BF16)
            s0 = s0_ref[si, h]
            o = jnp.dot(a, v[:, sl], preferred_element_type=F32)
            o = o + jnp.dot(qd[:, sl], s0.astype(BF16), preferred_element_type=F32)
            o_scr[rows, sl] = o
            decay_t = jnp.broadcast_to(decay[:, sl], (dk, dk)).T
            sout_ref[si, h] = s0 * decay_t + lax.dot_general(ke[:, sl], v[:, sl], TN_DIMS,
                                                             preferred_element_type=F32)
        return carry

    lax.fori_loop(0, n_seq, seq_body, 0, unroll=SEQ_UNROLL)
    _merge_store(m_ref, o_scr[...], ng_ref, gg_ref, a_ref)


def _hgrn_seq(ng, qd, kd, dec, hi, gg, a, sinit, *, rb, chunk):
    t, d_h = hi.shape
    n_heads = sinit.shape[0]
    return pl.pallas_call(
        functools.partial(_hgrn_seq_kernel, chunk=chunk, n_heads=n_heads),
        grid=(t // rb,),
        in_specs=[_small(ng), _rows(d_h, rb), _rows(d_h, rb), _rows(d_h, rb // chunk), _rows(d_h, rb),
                  _rows(d_h, rb), _rows(d_h, rb), _small(sinit)],
        out_specs=[_rows(d_h, rb), _small(sinit)],
        out_shape=(jax.ShapeDtypeStruct((t, d_h), a.dtype), jax.ShapeDtypeStruct(sinit.shape, F32)),
        scratch_shapes=[pltpu.VMEM(sinit.shape, F32), pltpu.VMEM((rb, d_h), F32)],
        compiler_params=_cparams(),
        name="hgrn_seq",
    )(ng, qd, kd, dec, hi, gg, a, sinit)


def _hgrn_sample(ng, qd, kd, dec, hi, gg, a, s0, *, seq_block):
    t, d_h = hi.shape
    n_seq = s0.shape[0]
    rb = seq_block * (t // n_seq)
    state = pl.BlockSpec((seq_block,) + s0.shape[1:], lambda i: (i, 0, 0, 0))
    return pl.pallas_call(
        _hgrn_sample_kernel,
        grid=(n_seq // seq_block,),
        in_specs=[_small(ng), _rows(d_h, rb), _rows(d_h, rb), _rows(d_h, seq_block), _rows(d_h, rb),
                  _rows(d_h, rb), _rows(d_h, rb), state],
        out_specs=[_rows(d_h, rb), state],
        out_shape=(jax.ShapeDtypeStruct((t, d_h), a.dtype), jax.ShapeDtypeStruct(s0.shape, F32)),
        scratch_shapes=[pltpu.VMEM((rb, d_h), F32)],
        compiler_params=_cparams(),
        name="hgrn_sample",
    )(ng, qd, kd, dec, hi, gg, a, s0)


def _layer_norm(v, g_ref, b_ref):
    mu = jnp.mean(v, axis=-1, keepdims=True)
    c = v - mu
    var = jnp.mean(c * c, axis=-1, keepdims=True)
    return c * lax.rsqrt(var + LN_EPS) * g_ref[...] + b_ref[...]


def _outffn_kernel(x_ref, m_ref, hist_ref, wo_ref, g1_ref, b1_ref, wu_ref, cw_ref, cb_ref, wd_ref,
                   g2_ref, b2_ref, y_ref, tail_ref, ubuf, h_scr, *, alpha, shift):
    i = pl.program_id(0)
    tb = x_ref.shape[0]
    d_ff = wd_ref.shape[0]
    nh = hist_ref.shape[0]
    hpad = ubuf.shape[0] - tb

    @pl.when(i == 0)
    def _():
        ubuf[hpad - nh:hpad, :] = hist_ref[...]

    x = x_ref[...]
    mix = jnp.dot(m_ref[...].astype(BF16), wo_ref[...], preferred_element_type=F32)
    y1 = _layer_norm(alpha * x + mix, g1_ref, b1_ref)
    y1b = y1.astype(BF16)

    for c0 in range(0, d_ff, FF_COLS):
        cols = slice(c0, c0 + FF_COLS)
        u = jnp.dot(y1b, wu_ref[:, cols], preferred_element_type=F32)
        val = jnp.dot(y1b, wu_ref[:, d_ff + c0:d_ff + c0 + FF_COLS], preferred_element_type=F32)
        ubuf[hpad:hpad + tb, cols] = u
        u1 = ubuf[hpad - shift:hpad - shift + tb, cols]
        u2 = ubuf[hpad - 2 * shift:hpad - 2 * shift + tb, cols]
        conv = cb_ref[:, cols] + cw_ref[0:1, cols] * u2 + cw_ref[1:2, cols] * u1 + cw_ref[2:3, cols] * u
        h_scr[:, cols] = (jax.nn.gelu(conv) * val).astype(BF16)

    ffn = jnp.dot(h_scr[...], wd_ref[...], preferred_element_type=F32)
    y_ref[...] = _layer_norm(alpha * y1 + ffn, g2_ref, b2_ref)
    tail = ubuf[hpad + tb - nh:hpad + tb, :]
    tail_ref[...] = tail
    ubuf[hpad - nh:hpad, :] = tail


def _outffn(x, m, hist, wo, g1, b1, wu, cw, cb, wd, g2, b2, *, tb, alpha, shift):
    t, d = x.shape
    d_ff = wd.shape[0]
    assert t % tb == 0 and d_ff % FF_COLS == 0 and hist.shape == (2 * shift, d_ff) and tb >= 2 * shift
    hpad = -(-2 * shift // 8) * 8
    return pl.pallas_call(
        functools.partial(_outffn_kernel, alpha=alpha, shift=shift),
        grid=(t // tb,),
        in_specs=[_rows(d, tb), _rows(d, tb), _small(hist), _resident(wo.shape), _small(g1), _small(b1),
                  _resident(wu.shape), _small(cw), _small(cb), _resident(wd.shape), _small(g2), _small(b2)],
        out_specs=[_rows(d, tb), _small(hist)],
        out_shape=(jax.ShapeDtypeStruct((t, d), F32), jax.ShapeDtypeStruct(hist.shape, F32)),
        scratch_shapes=[pltpu.VMEM((hpad + tb, d_ff), F32), pltpu.VMEM((tb, d_ff), BF16)],
        compiler_params=_cparams(),
        name="outffn",
    )(x, m, hist, wo, g1, b1, wu, cw, cb, wd, g2, b2)


def kernel(x_prompt, x_sample, cache_meta_k, cache_meta_v, cache_win_k, cache_win_v, state_hgrn, state_conv,
           meta_tokens, w_in, attn_sinks, hgrn_lb, hgrn_norm_g, w_out, ln1_g, ln1_b,
           w_up, conv_w, conv_b, w_down, ln2_g, ln2_b):
    depth = w_in.shape[0]
    assert depth == 1, "only a single layer is supported"
    layer = 0
    batch, seq, d = x_prompt.shape
    assert batch == 1
    n_seq, t_new, _ = x_sample.shape
    _, _, n_meta, n_kv, head_dim = cache_meta_k.shape
    window = cache_win_k.shape[2]
    n_heads = attn_sinks.shape[1]
    _, _, hg_heads, hg_k, hg_v = state_hgrn.shape
    d_ff = w_down.shape[1]
    d_attn = n_heads * head_dim
    d_kv = n_kv * head_dim
    d_h = hg_heads * hg_k
    assert head_dim == HALF and n_heads == 4 * n_kv and n_kv % 2 == 0 and hg_k == hg_v == LANES
    assert d_attn == d and d_h == d and window == LANES and seq % ROW_BLOCK == 0
    alpha = (2.0 * depth) ** 0.25

    wi = w_in[layer].astype(BF16)
    wo = w_out[layer].astype(BF16)
    wu = w_up[layer].astype(BF16)
    wd = w_down[layer].astype(BF16)
    sinks = attn_sinks[layer].astype(F32)
    ng = hgrn_norm_g[layer].reshape(1, d_h)
    g1, b1 = ln1_g[layer].reshape(1, d), ln1_b[layer].reshape(1, d)
    g2, b2 = ln2_g[layer].reshape(1, d), ln2_b[layer].reshape(1, d)
    cw, cb = conv_w[layer], conv_b[layer].reshape(1, d_ff)

    inproj = functools.partial(_inproj, hgrn_lb, w=wi, d_attn=d_attn, d_kv2=2 * d_kv, d_h=d_h,
                               q_scale=head_dim ** -0.5 * LOG2E, layer=layer)
    outffn = functools.partial(_outffn, wo=wo, g1=g1, b1=b1, wu=wu, cw=cw, cb=cb, wd=wd, g2=g2, b2=b2, alpha=alpha)

    xm = meta_tokens.astype(F32)
    q_m, kv_m, qd_m, kd_m, dec_m, hi_m, sga_m, gg_m = inproj(xm, tb=n_meta, chunk=n_meta, act_dtype=BF16)
    a_m = _attn_meta(sinks, q_m, kv_m, sga_m, n_kv=n_kv)
    merged_m, st_m = _hgrn_seq(ng, qd_m, kd_m, dec_m, hi_m, gg_m, a_m, jnp.zeros((hg_heads, hg_v, hg_k), F32),
                               rb=n_meta, chunk=n_meta)
    _, tail_m = outffn(xm, merged_m, jnp.zeros((2, d_ff), F32), tb=n_meta, shift=1)

    xp = x_prompt[0]
    q_p, kv_p, qd_p, kd_p, dec_p, hi_p, sga_p, gg_p = inproj(xp, tb=ROW_BLOCK, chunk=HG_CHUNK, act_dtype=BF16)
    merged_p, st_p = _mixer_prompt(sinks, ng, q_p, kv_p, sga_p, kv_m, qd_p, kd_p, dec_p, hi_p, gg_p, st_m,
                                   rb=ROW_BLOCK, n_kv=n_kv, window=window, chunk=HG_CHUNK)
    y_p, tail_p = outffn(xp, merged_p, tail_m, tb=ROW_BLOCK, shift=1)

    xs = x_sample.reshape(n_seq * t_new, d)
    tb_s = min(ROW_BLOCK, n_seq * t_new)
    q_s, kv_s, qd_s, kd_s, dec_s, hi_s, sga_s, gg_s = inproj(xs, tb=tb_s, chunk=t_new, act_dtype=F32)
    a_s, wk_s, wv_s = _attn_sample(
        sinks, q_s, kv_s, sga_s,
        cache_meta_k[layer].reshape(n_seq, n_meta, d_kv), cache_meta_v[layer].reshape(n_seq, n_meta, d_kv),
        cache_win_k[layer].reshape(n_seq, window, d_kv), cache_win_v[layer].reshape(n_seq, window, d_kv),
        n_kv=n_kv, seq_block=8)
    merged_s, st_s = _hgrn_sample(ng, qd_s, kd_s, dec_s, hi_s, gg_s, a_s, state_hgrn[layer], seq_block=8)
    time_major = lambda a: a.reshape(n_seq, t_new, d).transpose(1, 0, 2).reshape(t_new * n_seq, d)
    hist_s = state_conv[layer].transpose(1, 0, 2).reshape(2 * n_seq, d_ff)
    y_s, tail_s = outffn(time_major(xs), time_major(merged_s), hist_s, tb=tb_s, shift=n_seq)

    kv_heads = lambda a, rows: a.reshape(1, -1, rows, n_kv, head_dim)
    return (
        y_p[None],
        y_s.reshape(t_new, n_seq, d).transpose(1, 0, 2),
        kv_heads(kv_m[:, :d_kv], n_meta),
        kv_heads(kv_m[:, d_kv:], n_meta),
        kv_heads(kv_p[seq - window:, :d_kv], window),
        kv_heads(kv_p[seq - window:, d_kv:], window),
        st_p.transpose(0, 2, 1)[None, None],
        tail_p[None, None],
        kv_heads(wk_s, window),
        kv_heads(wv_s, window),
        st_s[None],
        tail_s.reshape(2, n_seq, d_ff).transpose(1, 0, 2)[None],
    )
```
